```python
import jax, jax.numpy as jnp
from jax import lax
import numpy as np

D_MODEL = 2048
BATCH = 4
SEQ = 4096
DEPTH = 2

A_HEADS = 8
A_HEAD_DIM = 128
A_ROT_DIM = A_HEAD_DIM // 4
IDX_HEADS = 16
IDX_DIM = 64
IDX_ROT_DIM = IDX_DIM // 4
TOPK_MAX = 256
MLA_HEADS = 8
MLA_NOPE = 128
MLA_ROPE = 64
MLA_V = 128
Q_LORA = 512
KV_LORA = 256
FFN_DIM = -(-8 * D_MODEL // (3 * 256)) * 256
ROPE_THETA = 500000.0
MLA_ROPE_THETA = 10000.0
ALPHA = (2 * DEPTH) ** 0.25
BETA = (8 * DEPTH) ** -0.25
LN_EPS = 1e-5
RMS_EPS = 1e-6
Q_BLOCK = 128
SPARSE_Q_BLOCK = 64
A_WIDTH = A_HEADS * A_HEAD_DIM
IN_SIZES = (A_WIDTH, A_WIDTH, A_WIDTH, IDX_HEADS * IDX_DIM, IDX_HEADS, IDX_DIM, Q_LORA, KV_LORA, MLA_ROPE)
IN_COLS = sum(IN_SIZES)

kernel_name = "hybrid_dsa_mla_deepnorm_block"


def _layer_norm(x, g, b):
    xf = x.astype(jnp.float32)
    mu = jnp.mean(xf, axis=-1, keepdims=True)
    var = jnp.mean(jnp.square(xf - mu), axis=-1, keepdims=True)
    y = (xf - mu) * lax.rsqrt(var + LN_EPS) * g.astype(jnp.float32) + b.astype(jnp.float32)
    return y.astype(x.dtype)


def _rms_norm(x, g):
    xf = x.astype(jnp.float32)
    y = xf * lax.rsqrt(jnp.mean(jnp.square(xf), axis=-1, keepdims=True) + RMS_EPS) * g.astype(jnp.float32)
    return y.astype(x.dtype)


def _rope(x, pos, rot_dim, theta):
    half = rot_dim // 2
    inv_freq = theta ** (-2.0 * jnp.arange(half, dtype=jnp.float32) / rot_dim)
    ang = pos.astype(jnp.float32)[..., None] * inv_freq
    cos = jnp.cos(ang)[:, :, None, :].astype(x.dtype)
    sin = jnp.sin(ang)[:, :, None, :].astype(x.dtype)
    x1, x2 = x[..., :half], x[..., half:rot_dim]
    return jnp.concatenate([x1 * cos - x2 * sin, x2 * cos + x1 * sin, x[..., rot_dim:]], axis=-1)


def _to_blocks(a, blk):
    b, s = a.shape[:2]
    return jnp.moveaxis(a.reshape((b, s // blk, blk) + a.shape[2:]), 1, 0)


def _from_blocks(a):
    n, b, blk = a.shape[:3]
    return jnp.moveaxis(a, 0, 1).reshape((b, n * blk) + a.shape[3:])


def _dsa_attention(q, k, v, q_idx, k_idx, w_idx):
    s_len = q.shape[1]
    topk = min(TOPK_MAX, s_len // 4)
    pos_s = jnp.arange(s_len, dtype=jnp.int32)
    scale = A_HEAD_DIM ** -0.5

    def block(args):
        qb, qib, wib, tb = args
        rel = jax.nn.relu(jnp.einsum('bthd,bsd->bths', qib, k_idx).astype(jnp.float32))
        score = jnp.einsum('bths,bth->bts', rel, wib.astype(jnp.float32))
        causal = tb[:, None] >= pos_s[None, :]
        score = jnp.where(causal[None], score, -jnp.inf)
        _, sel = lax.top_k(score, topk)
        k_sel = jax.vmap(lambda kb, ib: kb[ib])(k, sel)
        v_sel = jax.vmap(lambda vb, ib: vb[ib])(v, sel)
        logits = jnp.einsum('bthd,btkhd->bthk', qb, k_sel).astype(jnp.float32) * scale
        valid = (sel <= tb[None, :, None])[:, :, None, :]
        p = jax.nn.softmax(jnp.where(valid, logits, -jnp.inf), axis=-1)
        return jnp.einsum('bthk,btkhd->bthd', p.astype(v.dtype), v_sel)

    out = lax.map(block, (_to_blocks(q, SPARSE_Q_BLOCK), _to_blocks(q_idx, SPARSE_Q_BLOCK),
                          _to_blocks(w_idx, SPARSE_Q_BLOCK), pos_s.reshape(-1, SPARSE_Q_BLOCK)))
    return _from_blocks(out)


def _causal_attention(q, k, v, scale):
    s_len = q.shape[1]
    pos_s = jnp.arange(s_len, dtype=jnp.int32)

    def block(args):
        qb, tb = args
        logits = jnp.einsum('bthd,bshd->bhts', qb, k).astype(jnp.float32) * scale
        mask = tb[:, None] >= pos_s[None, :]
        p = jax.nn.softmax(jnp.where(mask[None, None], logits, -jnp.inf), axis=-1)
        return jnp.einsum('bhts,bshd->bthd', p.astype(v.dtype), v)

    out = lax.map(block, (_to_blocks(q, Q_BLOCK), pos_s.reshape(-1, Q_BLOCK)))
    return _from_blocks(out)


def _hybrid_mixer(x, positions, w_in, g_cq, g_ckv, w_uq, w_ukv, w_o):
    b, s, _ = x.shape
    proj = jnp.einsum('bsd,de->bse', x, w_in)
    offsets = np.cumsum(IN_SIZES)[:-1].tolist()
    qa, ka, va, qi, wi, ki, cq, ckv, kr = jnp.split(proj, offsets, axis=-1)
    qa = _rope(qa.reshape(b, s, A_HEADS, A_HEAD_DIM), positions, A_ROT_DIM, ROPE_THETA)
    ka = _rope(ka.reshape(b, s, A_HEADS, A_HEAD_DIM), positions, A_ROT_DIM, ROPE_THETA)
    va = va.reshape(b, s, A_HEADS, A_HEAD_DIM)
    qi = _rope(qi.reshape(b, s, IDX_HEADS, IDX_DIM), positions, IDX_ROT_DIM, ROPE_THETA)
    ki = _rope(ki[:, :, None, :], positions, IDX_ROT_DIM, ROPE_THETA)[:, :, 0, :]
    wi = wi * (IDX_HEADS ** -0.5 * IDX_DIM ** -0.5)
    out_a = _dsa_attention(qa, ka, va, qi, ki, wi).reshape(b, s, A_WIDTH)
    q_b = jnp.einsum('bsr,re->bse', _rms_norm(cq, g_cq), w_uq).reshape(b, s, MLA_HEADS, MLA_NOPE + MLA_ROPE)
    q_nope = q_b[..., :MLA_NOPE]
    q_pe = _rope(q_b[..., MLA_NOPE:], positions, MLA_ROPE, MLA_ROPE_THETA)
    kv = jnp.einsum('bsr,re->bse', _rms_norm(ckv, g_ckv), w_ukv).reshape(b, s, MLA_HEADS, MLA_NOPE + MLA_V)
    k_nope, v_b = kv[..., :MLA_NOPE], kv[..., MLA_NOPE:]
    k_pe = _rope(kr[:, :, None, :], positions, MLA_ROPE, MLA_ROPE_THETA)
    k_pe = jnp.broadcast_to(k_pe, (b, s, MLA_HEADS, MLA_ROPE))
    q_mla = jnp.concatenate([q_nope, q_pe], axis=-1)
    k_mla = jnp.concatenate([k_nope, k_pe], axis=-1)
    out_b = _causal_attention(q_mla, k_mla, v_b, (MLA_NOPE + MLA_ROPE) ** -0.5).reshape(b, s, MLA_HEADS * MLA_V)
    return jnp.einsum('bse,ed->bsd', jnp.concatenate([out_a, out_b], axis=-1), w_o)


def _swiglu(x, w_gate, w_up, w_down):
    h = jax.nn.silu(jnp.einsum('bsd,df->bsf', x, w_gate)) * jnp.einsum('bsd,df->bsf', x, w_up)
    return jnp.einsum('bsf,fd->bsd', h, w_down)


def setup_inputs(seed: int = 0) -> dict:
    key = jax.random.key(seed)
    ks = jax.random.split(key, 16)
    f32 = jnp.float32
    nrm = lambda k, shape, scale: jax.random.normal(k, shape, f32) * scale
    mix_width = A_WIDTH + MLA_HEADS * MLA_V
    return {
        "x": jax.random.normal(ks[0], (BATCH, SEQ, D_MODEL), f32),
        "positions": jnp.broadcast_to(jnp.arange(SEQ, dtype=jnp.int32), (BATCH, SEQ)),
        "w_in": nrm(ks[1], (DEPTH, D_MODEL, IN_COLS), D_MODEL ** -0.5),
        "g_cq": 1.0 + nrm(ks[2], (DEPTH, Q_LORA), 0.02),
        "g_ckv": 1.0 + nrm(ks[3], (DEPTH, KV_LORA), 0.02),
        "w_uq": nrm(ks[4], (DEPTH, Q_LORA, MLA_HEADS * (MLA_NOPE + MLA_ROPE)), Q_LORA ** -0.5),
        "w_ukv": nrm(ks[5], (DEPTH, KV_LORA, MLA_HEADS * (MLA_NOPE + MLA_V)), KV_LORA ** -0.5),
        "w_o": nrm(ks[6], (DEPTH, mix_width, D_MODEL), BETA * mix_width ** -0.5),
        "ln1_g": 1.0 + nrm(ks[7], (DEPTH, D_MODEL), 0.02),
        "ln1_b": nrm(ks[8], (DEPTH, D_MODEL), 0.02),
        "w_gate": nrm(ks[9], (DEPTH, D_MODEL, FFN_DIM), D_MODEL ** -0.5),
        "w_up": nrm(ks[10], (DEPTH, D_MODEL, FFN_DIM), D_MODEL ** -0.5),
        "w_down": nrm(ks[11], (DEPTH, FFN_DIM, D_MODEL), BETA * FFN_DIM ** -0.5),
        "ln2_g": 1.0 + nrm(ks[12], (DEPTH, D_MODEL), 0.02),
        "ln2_b": nrm(ks[13], (DEPTH, D_MODEL), 0.02),
    }


def reference(x, positions, w_in, g_cq, g_ckv, w_uq, w_ukv, w_o, ln1_g, ln1_b,
              w_gate, w_up, w_down, ln2_g, ln2_b):
    for i in range(DEPTH):
        mix = _hybrid_mixer(x, positions, w_in[i], g_cq[i], g_ckv[i], w_uq[i], w_ukv[i], w_o[i])
        x = _layer_norm(ALPHA * x + mix, ln1_g[i], ln1_b[i])
        ffn = _swiglu(x, w_gate[i], w_up[i], w_down[i])
        x = _layer_norm(ALPHA * x + ffn, ln2_g[i], ln2_b[i])
    return x
```

```python
import functools

import numpy as np
import jax
import jax.numpy as jnp
from jax import lax
from jax.experimental import pallas as pl
from jax.experimental.pallas import tpu as pltpu

F32 = jnp.float32
BF16 = jnp.bfloat16

D_MODEL = 2048
A_HEADS = 8
A_HEAD_DIM = 128
A_ROT_DIM = A_HEAD_DIM // 4
IDX_HEADS = 16
IDX_DIM = 64
IDX_ROT_DIM = IDX_DIM // 4
TOPK_MAX = 256
MLA_HEADS = 8
MLA_NOPE = 128
MLA_ROPE = 64
MLA_V = 128
MLA_QK_PAD = 256
Q_LORA = 512
KV_LORA = 256
FFN_DIM = 5632
ROPE_THETA = 500000.0
MLA_ROPE_THETA = 10000.0
LN_EPS = 1e-5
RMS_EPS = 1e-6
A_WIDTH = A_HEADS * A_HEAD_DIM
LANES = 128
MASK_NEG = -1e30
VMEM_LIMIT = 56 * 1024 * 1024

_OFF = np.cumsum([0, A_WIDTH, A_WIDTH, A_WIDTH, IDX_HEADS * IDX_DIM, IDX_HEADS, IDX_DIM,
                  Q_LORA, KV_LORA, MLA_ROPE]).tolist()
MISC_W = 1024
MISC_CQ = 0
MISC_CKV = Q_LORA
MISC_KR = Q_LORA + KV_LORA
MISC_KI = MISC_KR + LANES


def _params(*sem):
    return pltpu.CompilerParams(dimension_semantics=sem, vmem_limit_bytes=VMEM_LIMIT)


def _rope_tables(positions, rot_dim, theta, period):
    half = rot_dim // 2
    inv_freq = theta ** (-2.0 * jnp.arange(half, dtype=F32) / rot_dim)
    ang = positions.astype(F32)[..., None] * inv_freq
    cos = jnp.cos(ang).reshape(-1, half)
    sin = jnp.sin(ang).reshape(-1, half)
    t = cos.shape[0]
    rest = period - rot_dim
    z_h = jnp.zeros((t, half), F32)
    z_r = jnp.zeros((t, rest), F32)
    c = jnp.concatenate([cos, cos, jnp.ones((t, rest), F32)], axis=1)
    s1 = jnp.concatenate([-sin, z_h, z_r], axis=1)
    s2 = jnp.concatenate([z_h, sin, z_r], axis=1)
    rep = LANES // period
    return tuple(jnp.tile(a, (1, rep)) for a in (c, s1, s2))


def _rope_chunk(a, c, s1, s2, half):
    return a * c + pltpu.roll(a, LANES - half, 1) * s1 + pltpu.roll(a, half, 1) * s2


def _mm_kernel(x_ref, w_ref, o_ref):
    acc = jnp.dot(x_ref[...].astype(BF16), w_ref[...], preferred_element_type=F32)
    o_ref[...] = acc.astype(o_ref.dtype)


def _mm_rope_kernel(x_ref, w_ref, c_ref, s1_ref, s2_ref, o_ref, *, half):
    acc = jnp.dot(x_ref[...].astype(BF16), w_ref[...], preferred_element_type=F32)
    c, s1, s2 = c_ref[...], s1_ref[...], s2_ref[...]
    for k in range(acc.shape[1] // LANES):
        sl = slice(k * LANES, (k + 1) * LANES)
        o_ref[:, sl] = _rope_chunk(acc[:, sl], c, s1, s2, half).astype(o_ref.dtype)


def _project(x, w, out_dtype, tables=None, half=0, tm=512, tn=1024):
    t, kdim = x.shape
    n = w.shape[1]
    grid = (t // tm, n // tn)
    x_spec = pl.BlockSpec((tm, kdim), lambda i, j: (i, 0))
    w_spec = pl.BlockSpec((kdim, tn), lambda i, j: (0, j))
    o_spec = pl.BlockSpec((tm, tn), lambda i, j: (i, j))
    if tables is None:
        return pl.pallas_call(
            _mm_kernel, grid=grid, in_specs=[x_spec, w_spec], out_specs=o_spec,
            out_shape=jax.ShapeDtypeStruct((t, n), out_dtype),
            compiler_params=_params("parallel", "arbitrary"))(x, w)
    t_spec = pl.BlockSpec((tm, LANES), lambda i, j: (i, 0))
    return pl.pallas_call(
        functools.partial(_mm_rope_kernel, half=half), grid=grid,
        in_specs=[x_spec, w_spec, t_spec, t_spec, t_spec], out_specs=o_spec,
        out_shape=jax.ShapeDtypeStruct((t, n), out_dtype),
        compiler_params=_params("parallel", "arbitrary"))(x, w, *tables)


def _mla_prep_kernel(misc_ref, gq_ref, gkv_ref, wq_ref, wk_ref, wv_ref,
                     cm_ref, s1m_ref, s2m_ref, ci_ref, s1i_ref, s2i_ref,
                     q_ref, k_ref, v_ref, ki_ref, wi_ref):
    tm = misc_ref.shape[0]
    cm, s1m, s2m = cm_ref[...], s1m_ref[...], s2m_ref[...]
    lane = lax.broadcasted_iota(jnp.int32, (tm, LANES), 1)

    cq = misc_ref[:, MISC_CQ:MISC_CQ + Q_LORA]
    cqn = cq * lax.rsqrt(jnp.mean(jnp.square(cq), axis=-1, keepdims=True) + RMS_EPS) * gq_ref[...]
    q_all = jnp.dot(cqn.astype(BF16), wq_ref[...], preferred_element_type=F32)
    for h in range(MLA_HEADS):
        base = h * MLA_QK_PAD
        q_ref[:, base:base + MLA_NOPE] = q_all[:, base:base + MLA_NOPE].astype(BF16)
        pe = q_all[:, base + MLA_NOPE:base + MLA_QK_PAD]
        q_ref[:, base + MLA_NOPE:base + MLA_QK_PAD] = _rope_chunk(pe, cm, s1m, s2m, MLA_ROPE // 2).astype(BF16)

    ckv = misc_ref[:, MISC_CKV:MISC_CKV + KV_LORA]
    ckvn = (ckv * lax.rsqrt(jnp.mean(jnp.square(ckv), axis=-1, keepdims=True) + RMS_EPS) * gkv_ref[...]).astype(BF16)
    kn = jnp.dot(ckvn, wk_ref[...], preferred_element_type=F32)
    v_ref[...] = jnp.dot(ckvn, wv_ref[...], preferred_element_type=F32).astype(BF16)

    krc = misc_ref[:, MISC_KR:MISC_KR + LANES]
    kpe = jnp.where(lane < MLA_ROPE, _rope_chunk(krc, cm, s1m, s2m, MLA_ROPE // 2), 0.0).astype(BF16)
    for h in range(MLA_HEADS):
        base = h * MLA_QK_PAD
        k_ref[:, base:base + MLA_NOPE] = kn[:, h * MLA_NOPE:(h + 1) * MLA_NOPE].astype(BF16)
        k_ref[:, base + MLA_NOPE:base + MLA_QK_PAD] = kpe

    kic = misc_ref[:, MISC_KI:MISC_KI + LANES]
    ki_lo = _rope_chunk(kic, ci_ref[...], s1i_ref[...], s2i_ref[...], IDX_ROT_DIM // 2)
    ki_ref[:, 0:LANES] = ki_lo.astype(BF16)
    ki_ref[:, LANES:2 * LANES] = pltpu.roll(ki_lo, IDX_DIM, 1).astype(BF16)
    wi_ref[...] = krc[:, MLA_ROPE:MLA_ROPE + IDX_HEADS] * (IDX_HEADS ** -0.5 * IDX_DIM ** -0.5)


def _mla_prep(misc, gq, gkv, wq, wk, wv, tab_m, tab_i, tm=256):
    t = misc.shape[0]
    row = lambda w: pl.BlockSpec((tm, w), lambda i: (i, 0))
    full = lambda a: pl.BlockSpec(a.shape, lambda i: (0, 0))
    return pl.pallas_call(
        _mla_prep_kernel, grid=(t // tm,),
        in_specs=[row(MISC_W), full(gq), full(gkv), full(wq), full(wk), full(wv)] + [row(LANES)] * 6,
        out_specs=[row(MLA_HEADS * MLA_QK_PAD), row(MLA_HEADS * MLA_QK_PAD), row(MLA_HEADS * MLA_V),
                   row(2 * LANES), row(IDX_HEADS)],
        out_shape=[jax.ShapeDtypeStruct((t, MLA_HEADS * MLA_QK_PAD), BF16),
                   jax.ShapeDtypeStruct((t, MLA_HEADS * MLA_QK_PAD), BF16),
                   jax.ShapeDtypeStruct((t, MLA_HEADS * MLA_V), BF16),
                   jax.ShapeDtypeStruct((t, 2 * LANES), BF16),
                   jax.ShapeDtypeStruct((t, IDX_HEADS), F32)],
        compiler_params=_params("parallel"))(misc, gq, gkv, wq, wk, wv, *tab_m, *tab_i)


def _indexer_kernel(q_ref, w_ref, k_ref, bias_ref, sc_ref, *, tq, topk, max_iter):
    i = pl.program_id(1)
    nblk = bias_ref.shape[2]
    nb = i + 1
    kf = float(topk)
    row = i * tq + lax.broadcasted_iota(jnp.int32, (tq, 1), 0)
    col0 = lax.broadcasted_iota(jnp.int32, (1, tq), 1)
    wi = w_ref[...]
    nt = (((1,), (1,)), ((), ()))

    def score_blk(j, carry):
        start = pl.multiple_of(j * tq, tq)
        k_lo = k_ref[pl.ds(start, tq), 0:LANES]
        k_hi = k_ref[pl.ds(start, tq), LANES:2 * LANES]
        acc = jnp.zeros((tq, tq), F32)
        for p in range(IDX_HEADS // 2):
            qp = q_ref[:, p * LANES:(p + 1) * LANES]
            s0 = lax.dot_general(qp, k_lo, nt, preferred_element_type=F32)
            s1 = lax.dot_general(qp, k_hi, nt, preferred_element_type=F32)
            acc = acc + jnp.maximum(s0, 0.0) * wi[:, 2 * p:2 * p + 1]
            acc = acc + jnp.maximum(s1, 0.0) * wi[:, 2 * p + 1:2 * p + 2]
        causal = (j * tq + col0) <= row
        sc_ref[j] = jnp.where(causal, acc + 0.0, -jnp.inf)
        return carry

    lax.fori_loop(0, nb, score_blk, 0)

    def lane_fold(x):
        out = x[:, 0:LANES]
        for c in range(1, tq // LANES):
            out = out + x[:, c * LANES:(c + 1) * LANES]
        return out

    def count_ge(th):
        def body(j, c):
            return c + lane_fold(jnp.where(sc_ref[j] >= th, 1.0, 0.0))
        part = lax.fori_loop(0, nb, body, jnp.zeros((tq, LANES), F32))
        return jnp.sum(part, axis=1, keepdims=True)

    def minmax(j, c):
        mx, mn = c
        s = sc_ref[j]
        mx = jnp.maximum(mx, jnp.max(s, axis=1, keepdims=True))
        mn = jnp.minimum(mn, jnp.min(jnp.where(s == -jnp.inf, jnp.inf, s), axis=1, keepdims=True))
        return mx, mn

    rmax, rmin = lax.fori_loop(0, nb, minmax, (jnp.full((tq, 1), -jnp.inf, F32), jnp.full((tq, 1), jnp.inf, F32)))

    c_all = count_ge(rmin)
    c_top = count_ge(rmax)
    top_full = c_top >= kf
    lo = jnp.where(top_full, rmax, rmin)
    c_lo = jnp.where(top_full, c_top, c_all)
    hi = jnp.where(top_full, jnp.inf, rmax)
    c_hi = jnp.where(top_full, 0.0, c_top)
    done = jnp.where(top_full | (c_lo <= kf), 1.0, 0.0)

    def cond(st):
        it, _, _, _, _, dn = st
        return jnp.logical_and(it < max_iter, jnp.min(dn) < 0.5)

    def step(st):
        it, lo, hi, c_lo, c_hi, dn = st
        mid = 0.5 * lo + 0.5 * hi
        stuck = (mid <= lo) | (mid >= hi)
        c = count_ge(mid)
        up = (c >= kf) & jnp.logical_not(stuck)
        down = (c < kf) & jnp.logical_not(stuck)
        lo = jnp.where(up, mid, lo)
        c_lo = jnp.where(up, c, c_lo)
        hi = jnp.where(down, mid, hi)
        c_hi = jnp.where(down, c, c_hi)
        dn = jnp.where(stuck | (c_lo <= kf), 1.0, dn)
        return it + 1, lo, hi, c_lo, c_hi, dn

    _, lo, hi, c_lo, c_hi, _ = lax.while_loop(cond, step, (jnp.int32(0), lo, hi, c_lo, c_hi, done))

    need = kf - c_hi
    last_col = jnp.full((tq, 1), nblk * tq - 1, jnp.int32)

    def band_prefix(jcut):
        def body(j, c):
            s = sc_ref[j]
            col = j * tq + col0
            hit = (s >= lo) & (s < hi) & (col < jcut)
            return c + lane_fold(jnp.where(hit, 1.0, 0.0))
        part = lax.fori_loop(0, nb, body, jnp.zeros((tq, LANES), F32))
        return jnp.sum(part, axis=1, keepdims=True)

    def tie_search(_):
        nbits = int(nblk * tq - 1).bit_length()
        jcut = jnp.zeros((tq, 1), jnp.int32)
        for b in range(nbits - 1, -1, -1):
            trial = jcut | (1 << b)
            jcut = jnp.where(band_prefix(trial) < need, trial, jcut)
        return jcut

    has_excess = jnp.max(c_lo) > kf
    jcut = lax.cond(has_excess, tie_search, lambda _: last_col, 0)
    jcut = jnp.where(c_lo > kf, jcut, last_col)

    def emit(j, carry):
        s = sc_ref[j]
        col = j * tq + col0
        sel = (s >= lo) & ((s >= hi) | (col <= jcut))
        bias_ref[0, 0, j] = jnp.where(sel, 0.0, MASK_NEG).astype(bias_ref.dtype)
        return carry

    lax.fori_loop(0, nb, emit, 0)

    def fill(j, carry):
        bias_ref[0, 0, j] = jnp.full((tq, tq), MASK_NEG, bias_ref.dtype)
        return carry

    lax.fori_loop(nb, nblk, fill, 0)


def _indexer(qi, wi, ki2, batch, seq, topk, tq=256):
    nblk = seq // tq
    kern = functools.partial(_indexer_kernel, tq=tq, topk=topk, max_iter=64)
    return pl.pallas_call(
        kern, grid=(batch, nblk),
        in_specs=[pl.BlockSpec((tq, IDX_HEADS * IDX_DIM), lambda b, i: (b * nblk + i, 0)),
                  pl.BlockSpec((tq, IDX_HEADS), lambda b, i: (b * nblk + i, 0)),
                  pl.BlockSpec((seq, 2 * LANES), lambda b, i: (b, 0))],
        out_specs=pl.BlockSpec((1, 1, nblk, tq, tq), lambda b, i: (b, i, 0, 0, 0)),
        out_shape=jax.ShapeDtypeStruct((batch, nblk, nblk, tq, tq), BF16),
        scratch_shapes=[pltpu.VMEM((nblk, tq, tq), F32)],
        compiler_params=_params("parallel", "arbitrary"))(qi, wi, ki2)


def _attn_kernel(it_ref, jt_ref, q_ref, k_ref, v_ref, *rest, heads, dqk, dv, scale, use_bias):
    if use_bias:
        bias_ref, o_ref, m_ref, l_ref, acc_ref = rest
    else:
        o_ref, m_ref, l_ref, acc_ref = rest
    step = pl.program_id(1)
    i = it_ref[step]
    j = jt_ref[step]
    tq = q_ref.shape[0]
    tk = k_ref.shape[0]
    nt = (((1,), (1,)), ((), ()))

    @pl.when(j == 0)
    def _():
        m_ref[...] = jnp.full(m_ref.shape, -jnp.inf, F32)
        l_ref[...] = jnp.zeros(l_ref.shape, F32)
        acc_ref[...] = jnp.zeros(acc_ref.shape, F32)

    if use_bias:
        bias = bias_ref[0, 0, 0].astype(F32)
    else:
        row = i * tq + lax.broadcasted_iota(jnp.int32, (tq, 1), 0)
        col = j * tk + lax.broadcasted_iota(jnp.int32, (1, tk), 1)
        causal = col <= row

    for h in range(heads):
        q = q_ref[:, h * dqk:(h + 1) * dqk]
        k = k_ref[:, h * dqk:(h + 1) * dqk]
        v = v_ref[:, h * dv:(h + 1) * dv]
        s = lax.dot_general(q, k, nt, preferred_element_type=F32) * scale
        if use_bias:
            s = s + bias
        else:
            s = jnp.where(causal, s, MASK_NEG)
        m_prev = m_ref[h]
        m_new = jnp.maximum(m_prev, jnp.max(s, axis=1, keepdims=True))
        alpha = jnp.exp(m_prev - m_new)
        p = jnp.exp(s - m_new)
        l_ref[h] = alpha * l_ref[h] + jnp.sum(p, axis=1, keepdims=True)
        pv = jnp.dot(p.astype(BF16), v, preferred_element_type=F32)
        acc_ref[:, h * dv:(h + 1) * dv] = alpha * acc_ref[:, h * dv:(h + 1) * dv] + pv
        m_ref[h] = m_new

    @pl.when(j == i)
    def _():
        for h in range(heads):
            inv = 1.0 / l_ref[h]
            o_ref[:, h * dv:(h + 1) * dv] = (acc_ref[:, h * dv:(h + 1) * dv] * inv).astype(o_ref.dtype)


def _attention(q, k, v, bias, batch, seq, heads, dqk, dv, scale, q_col=0, k_col=0, tq=256):
    nblk = seq // tq
    pairs = [(i, j) for i in range(nblk) for j in range(i + 1)]
    it = jnp.asarray([p[0] for p in pairs], jnp.int32)
    jt = jnp.asarray([p[1] for p in pairs], jnp.int32)
    use_bias = bias is not None
    in_specs = [pl.BlockSpec((tq, heads * dqk), lambda b, s, it, jt: (b * nblk + it[s], q_col)),
                pl.BlockSpec((tq, heads * dqk), lambda b, s, it, jt: (b * nblk + jt[s], k_col)),
                pl.BlockSpec((tq, heads * dv), lambda b, s, it, jt: (b * nblk + jt[s], 0))]
    args = [q, k, v]
    if use_bias:
        in_specs.append(pl.BlockSpec((1, 1, 1, tq, tq), lambda b, s, it, jt: (b, it[s], jt[s], 0, 0)))
        args.append(bias)
    kern = functools.partial(_attn_kernel, heads=heads, dqk=dqk, dv=dv, scale=scale, use_bias=use_bias)
    return pl.pallas_call(
        kern,
        grid_spec=pltpu.PrefetchScalarGridSpec(
            num_scalar_prefetch=2, grid=(batch, len(pairs)), in_specs=in_specs,
            out_specs=pl.BlockSpec((tq, heads * dv), lambda b, s, it, jt: (b * nblk + it[s], 0)),
            scratch_shapes=[pltpu.VMEM((heads, tq, 1), F32), pltpu.VMEM((heads, tq, 1), F32),
                            pltpu.VMEM((tq, heads * dv), F32)]),
        out_shape=jax.ShapeDtypeStruct((batch * seq, heads * dv), BF16),
        compiler_params=_params("parallel", "arbitrary"))(it, jt, *args)


def _residual_ln(x, upd, g, b, alpha):
    y = alpha * x + upd
    mu = jnp.mean(y, axis=-1, keepdims=True)
    d = y - mu
    var = jnp.mean(jnp.square(d), axis=-1, keepdims=True)
    return d * lax.rsqrt(var + LN_EPS) * g + b


def _outproj_ln_kernel(a_ref, b_ref, wa_ref, wb_ref, x_ref, g_ref, beta_ref, o_ref, *, alpha):
    mix = jnp.dot(a_ref[...], wa_ref[...], preferred_element_type=F32)
    mix = mix + jnp.dot(b_ref[...], wb_ref[...], preferred_element_type=F32)
    o_ref[...] = _residual_ln(x_ref[...], mix, g_ref[...], beta_ref[...], alpha)


def _outproj_ln(a, b, wa, wb, x, g, beta, alpha, tm=512):
    t, d = x.shape
    row = lambda w: pl.BlockSpec((tm, w), lambda i: (i, 0))
    full = lambda arr: pl.BlockSpec(arr.shape, lambda i: (0, 0))
    return pl.pallas_call(
        functools.partial(_outproj_ln_kernel, alpha=alpha), grid=(t // tm,),
        in_specs=[row(a.shape[1]), row(b.shape[1]), full(wa), full(wb), row(d), full(g), full(beta)],
        out_specs=row(d), out_shape=jax.ShapeDtypeStruct((t, d), F32),
        compiler_params=_params("parallel"))(a, b, wa, wb, x, g, beta)


def _ffn_up_kernel(x_ref, wg_ref, wu_ref, h_ref):
    xb = x_ref[...].astype(BF16)
    g = jnp.dot(xb, wg_ref[...], preferred_element_type=F32)
    u = jnp.dot(xb, wu_ref[...], preferred_element_type=F32)
    h_ref[...] = (g * (1.0 / (1.0 + jnp.exp(-g))) * u).astype(h_ref.dtype)


def _ffn_up(x, wg, wu, tm=1024, tn=512):
    t, d = x.shape
    f = wg.shape[1]
    return pl.pallas_call(
        _ffn_up_kernel, grid=(t // tm, f // tn),
        in_specs=[pl.BlockSpec((tm, d), lambda i, j: (i, 0)),
                  pl.BlockSpec((d, tn), lambda i, j: (0, j)),
                  pl.BlockSpec((d, tn), lambda i, j: (0, j))],
        out_specs=pl.BlockSpec((tm, tn), lambda i, j: (i, j)),
        out_shape=jax.ShapeDtypeStruct((t, f), BF16),
        compiler_params=_params("parallel", "arbitrary"))(x, wg, wu)


def _ffn_down_ln_kernel(h_ref, w_ref, x_ref, g_ref, beta_ref, o_ref, acc_ref, *, alpha):
    k = pl.program_id(1)

    @pl.when(k == 0)
    def _():
        acc_ref[...] = jnp.zeros(acc_ref.shape, F32)

    acc_ref[...] += jnp.dot(h_ref[...], w_ref[...], preferred_element_type=F32)

    @pl.when(k == pl.num_programs(1) - 1)
    def _():
        o_ref[...] = _residual_ln(x_ref[...], acc_ref[...], g_ref[...], beta_ref[...], alpha)


def _ffn_down_ln(h, w, x, g, beta, alpha, tm=512, tk=1408):
    t, d = x.shape
    f = h.shape[1]
    return pl.pallas_call(
        functools.partial(_ffn_down_ln_kernel, alpha=alpha), grid=(t // tm, f // tk),
        in_specs=[pl.BlockSpec((tm, tk), lambda i, k: (i, k)),
                  pl.BlockSpec((tk, d), lambda i, k: (k, 0)),
                  pl.BlockSpec((tm, d), lambda i, k: (i, 0)),
                  pl.BlockSpec((1, d), lambda i, k: (0, 0)),
                  pl.BlockSpec((1, d), lambda i, k: (0, 0))],
        out_specs=pl.BlockSpec((tm, d), lambda i, k: (i, 0)),
        out_shape=jax.ShapeDtypeStruct((t, d), F32),
        scratch_shapes=[pltpu.VMEM((tm, d), F32)],
        compiler_params=_params("parallel", "arbitrary"))(h, w, x, g, beta)


def _pack_mixer_weights(w_in, w_uq, w_ukv):
    sl = lambda k: w_in[:, _OFF[k]:_OFF[k + 1]]
    qa, ka, va, qi, wi, ki, cq, ckv, kr = (sl(k) for k in range(9))
    d = w_in.shape[0]
    zeros = lambda n: jnp.zeros((d, n), w_in.dtype)
    w_misc = jnp.concatenate([cq, ckv, kr, wi, zeros(LANES - MLA_ROPE - IDX_HEADS), ki, zeros(LANES - IDX_DIM)], axis=1)
    w_qk = jnp.concatenate([qa, ka], axis=1)
    uq = w_uq.reshape(Q_LORA, MLA_HEADS, MLA_NOPE + MLA_ROPE)
    uq = jnp.pad(uq, ((0, 0), (0, 0), (0, MLA_QK_PAD - MLA_NOPE - MLA_ROPE))).reshape(Q_LORA, MLA_HEADS * MLA_QK_PAD)
    ukv = w_ukv.reshape(KV_LORA, MLA_HEADS, MLA_NOPE + MLA_V)
    uk = ukv[:, :, :MLA_NOPE].reshape(KV_LORA, MLA_HEADS * MLA_NOPE)
    uv = ukv[:, :, MLA_NOPE:].reshape(KV_LORA, MLA_HEADS * MLA_V)
    return tuple(a.astype(BF16) for a in (w_qk, va, qi, w_misc, uq, uk, uv))


def kernel(x, positions, w_in, g_cq, g_ckv, w_uq, w_ukv, w_o, ln1_g, ln1_b, w_gate, w_up, w_down, ln2_g, ln2_b):
    batch, seq, d = x.shape
    depth = w_in.shape[0]
    alpha = (2 * depth) ** 0.25
    topk = min(TOPK_MAX, seq // 4)
    tab_a = _rope_tables(positions, A_ROT_DIM, ROPE_THETA, A_HEAD_DIM)
    tab_i = _rope_tables(positions, IDX_ROT_DIM, ROPE_THETA, IDX_DIM)
    tab_m = _rope_tables(positions, MLA_ROPE, MLA_ROPE_THETA, LANES)
    xt = x.reshape(batch * seq, d)
    for l in range(depth):
        w_qk, w_v, w_qi, w_misc, uq, uk, uv = _pack_mixer_weights(w_in[l], w_uq[l], w_ukv[l])
        qk = _project(xt, w_qk, BF16, tab_a, A_ROT_DIM // 2)
        va = _project(xt, w_v, BF16)
        qi = _project(xt, w_qi, BF16, tab_i, IDX_ROT_DIM // 2)
        misc = _project(xt, w_misc, F32)
        q_mla, k_mla, v_mla, ki2, wi = _mla_prep(
            misc, g_cq[l].reshape(1, -1), g_ckv[l].reshape(1, -1), uq, uk, uv, tab_m, tab_i)
        bias = _indexer(qi, wi, ki2, batch, seq, topk)
        out_a = _attention(qk, qk, va, bias, batch, seq,
                           A_HEADS, A_HEAD_DIM, A_HEAD_DIM, A_HEAD_DIM ** -0.5, q_col=0, k_col=1)
        out_b = _attention(q_mla, k_mla, v_mla, None, batch, seq,
                           MLA_HEADS, MLA_QK_PAD, MLA_V, (MLA_NOPE + MLA_ROPE) ** -0.5)
        wo = w_o[l].astype(BF16)
        xt = _outproj_ln(out_a, out_b, wo[:A_WIDTH], wo[A_WIDTH:], xt,
                         ln1_g[l].reshape(1, -1), ln1_b[l].reshape(1, -1), alpha)
        h = _ffn_up(xt, w_gate[l].astype(BF16), w_up[l].astype(BF16))
        xt = _ffn_down_ln(h, w_down[l].astype(BF16), xt,
                          ln2_g[l].reshape(1, -1), ln2_b[l].reshape(1, -1), alpha)
    return xt.reshape(batch, seq, d)
```

```python
import functools
import math

import numpy as np
import jax
import jax.numpy as jnp
from jax import lax
from jax.experimental import pallas as pl
from jax.experimental.pallas import tpu as pltpu

F32 = jnp.float32
BF16 = jnp.bfloat16

D_MODEL = 2048
A_HEADS = 8
A_HEAD_DIM = 128
A_ROT_DIM = A_HEAD_DIM // 4
IDX_HEADS = 16
IDX_DIM = 64
IDX_ROT_DIM = IDX_DIM // 4
TOPK_MAX = 256
MLA_HEADS = 8
MLA_NOPE = 128
MLA_ROPE = 64
MLA_V = 128
MLA_QK_PAD = 256
Q_LORA = 512
KV_LORA = 256
FFN_DIM = 5632
ROPE_THETA = 500000.0
MLA_ROPE_THETA = 10000.0
LN_EPS = 1e-5
RMS_EPS = 1e-6
A_WIDTH = A_HEADS * A_HEAD_DIM
LANES = 128
SUBLANES = 8
MASK_NEG = -1e30
VMEM_LIMIT = 56 * 1024 * 1024
LOG2E = math.log2(math.e)
A_QSCALE = A_HEAD_DIM ** -0.5 * LOG2E
MLA_QSCALE = (MLA_NOPE + MLA_ROPE) ** -0.5 * LOG2E
ATTN_TQ = 256
ATTN_TK = 512

_OFF = np.cumsum([0, A_WIDTH, A_WIDTH, A_WIDTH, IDX_HEADS * IDX_DIM, IDX_HEADS, IDX_DIM,
                  Q_LORA, KV_LORA, MLA_ROPE]).tolist()
MISC_W = 1024
MISC_CQ = 0
MISC_CKV = Q_LORA
MISC_KR = Q_LORA + KV_LORA
MISC_KI = MISC_KR + LANES


def _params(*sem):
    return pltpu.CompilerParams(dimension_semantics=sem, vmem_limit_bytes=VMEM_LIMIT)


def _rope_tables(positions, rot_dim, theta, period):
    half = rot_dim // 2
    inv_freq = theta ** (-2.0 * jnp.arange(half, dtype=F32) / rot_dim)
    ang = positions.astype(F32)[..., None] * inv_freq
    cos = jnp.cos(ang).reshape(-1, half)
    sin = jnp.sin(ang).reshape(-1, half)
    t = cos.shape[0]
    rest = period - rot_dim
    z_h = jnp.zeros((t, half), F32)
    z_r = jnp.zeros((t, rest), F32)
    c = jnp.concatenate([cos, cos, jnp.ones((t, rest), F32)], axis=1)
    s1 = jnp.concatenate([-sin, z_h, z_r], axis=1)
    s2 = jnp.concatenate([z_h, sin, z_r], axis=1)
    rep = LANES // period
    return tuple(jnp.tile(a, (1, rep)) for a in (c, s1, s2))


def _rope_chunk(a, c, s1, s2, half):
    return a * c + pltpu.roll(a, LANES - half, 1) * s1 + pltpu.roll(a, half, 1) * s2


def _mm_kernel(x_ref, w_ref, o_ref):
    acc = jnp.dot(x_ref[...].astype(BF16), w_ref[...], preferred_element_type=F32)
    o_ref[...] = acc.astype(o_ref.dtype)


def _mm_rope_kernel(x_ref, w_ref, c_ref, s1_ref, s2_ref, o_ref, *, half, scale0):
    acc = jnp.dot(x_ref[...].astype(BF16), w_ref[...], preferred_element_type=F32)
    if scale0 is not None:
        acc = acc * jnp.where(pl.program_id(1) == 0, scale0, 1.0)
    c, s1, s2 = c_ref[...], s1_ref[...], s2_ref[...]
    for k in range(acc.shape[1] // LANES):
        sl = slice(k * LANES, (k + 1) * LANES)
        o_ref[:, sl] = _rope_chunk(acc[:, sl], c, s1, s2, half).astype(o_ref.dtype)


def _project(x, w, out_dtype, tables=None, half=0, scale0=None, tm=512, tn=1024):
    t, kdim = x.shape
    n = w.shape[1]
    grid = (t // tm, n // tn)
    x_spec = pl.BlockSpec((tm, kdim), lambda i, j: (i, 0))
    w_spec = pl.BlockSpec((kdim, tn), lambda i, j: (0, j))
    o_spec = pl.BlockSpec((tm, tn), lambda i, j: (i, j))
    if tables is None:
        return pl.pallas_call(
            _mm_kernel, grid=grid, in_specs=[x_spec, w_spec], out_specs=o_spec,
            out_shape=jax.ShapeDtypeStruct((t, n), out_dtype),
            compiler_params=_params("parallel", "arbitrary"))(x, w)
    t_spec = pl.BlockSpec((tm, LANES), lambda i, j: (i, 0))
    return pl.pallas_call(
        functools.partial(_mm_rope_kernel, half=half, scale0=scale0), grid=grid,
        in_specs=[x_spec, w_spec, t_spec, t_spec, t_spec], out_specs=o_spec,
        out_shape=jax.ShapeDtypeStruct((t, n), out_dtype),
        compiler_params=_params("parallel", "arbitrary"))(x, w, *tables)


def _mla_prep_kernel(misc_ref, gq_ref, gkv_ref, wq_ref, wk_ref, wv_ref,
                     cm_ref, s1m_ref, s2m_ref, ci_ref, s1i_ref, s2i_ref,
                     q_ref, k_ref, v_ref, ki_ref, wi_ref):
    tm = misc_ref.shape[0]
    cm, s1m, s2m = cm_ref[...], s1m_ref[...], s2m_ref[...]
    lane = lax.broadcasted_iota(jnp.int32, (tm, LANES), 1)

    cq = misc_ref[:, MISC_CQ:MISC_CQ + Q_LORA]
    cqn = cq * lax.rsqrt(jnp.mean(jnp.square(cq), axis=-1, keepdims=True) + RMS_EPS) * gq_ref[...]
    q_all = jnp.dot(cqn.astype(BF16), wq_ref[...], preferred_element_type=F32) * MLA_QSCALE
    for h in range(MLA_HEADS):
        base = h * MLA_QK_PAD
        q_ref[:, base:base + MLA_NOPE] = q_all[:, base:base + MLA_NOPE].astype(BF16)
        pe = q_all[:, base + MLA_NOPE:base + MLA_QK_PAD]
        q_ref[:, base + MLA_NOPE:base + MLA_QK_PAD] = _rope_chunk(pe, cm, s1m, s2m, MLA_ROPE // 2).astype(BF16)

    ckv = misc_ref[:, MISC_CKV:MISC_CKV + KV_LORA]
    ckvn = (ckv * lax.rsqrt(jnp.mean(jnp.square(ckv), axis=-1, keepdims=True) + RMS_EPS) * gkv_ref[...]).astype(BF16)
    kn = jnp.dot(ckvn, wk_ref[...], preferred_element_type=F32)
    v_ref[...] = jnp.dot(ckvn, wv_ref[...], preferred_element_type=F32).astype(BF16)

    krc = misc_ref[:, MISC_KR:MISC_KR + LANES]
    kpe = jnp.where(lane < MLA_ROPE, _rope_chunk(krc, cm, s1m, s2m, MLA_ROPE // 2), 0.0).astype(BF16)
    for h in range(MLA_HEADS):
        base = h * MLA_QK_PAD
        k_ref[:, base:base + MLA_NOPE] = kn[:, h * MLA_NOPE:(h + 1) * MLA_NOPE].astype(BF16)
        k_ref[:, base + MLA_NOPE:base + MLA_QK_PAD] = kpe

    kic = misc_ref[:, MISC_KI:MISC_KI + LANES]
    ki_lo = _rope_chunk(kic, ci_ref[...], s1i_ref[...], s2i_ref[...], IDX_ROT_DIM // 2)
    ki_ref[:, 0:LANES] = ki_lo.astype(BF16)
    ki_ref[:, LANES:2 * LANES] = pltpu.roll(ki_lo, IDX_DIM, 1).astype(BF16)
    wi_ref[...] = krc[:, MLA_ROPE:MLA_ROPE + IDX_HEADS] * (IDX_HEADS ** -0.5 * IDX_DIM ** -0.5)


def _mla_prep(misc, gq, gkv, wq, wk, wv, tab_m, tab_i, tm=256):
    t = misc.shape[0]
    row = lambda w: pl.BlockSpec((tm, w), lambda i: (i, 0))
    full = lambda a: pl.BlockSpec(a.shape, lambda i: (0, 0))
    return pl.pallas_call(
        _mla_prep_kernel, grid=(t // tm,),
        in_specs=[row(MISC_W), full(gq), full(gkv), full(wq), full(wk), full(wv)] + [row(LANES)] * 6,
        out_specs=[row(MLA_HEADS * MLA_QK_PAD), row(MLA_HEADS * MLA_QK_PAD), row(MLA_HEADS * MLA_V),
                   row(2 * LANES), row(IDX_HEADS)],
        out_shape=[jax.ShapeDtypeStruct((t, MLA_HEADS * MLA_QK_PAD), BF16),
                   jax.ShapeDtypeStruct((t, MLA_HEADS * MLA_QK_PAD), BF16),
                   jax.ShapeDtypeStruct((t, MLA_HEADS * MLA_V), BF16),
                   jax.ShapeDtypeStruct((t, 2 * LANES), BF16),
                   jax.ShapeDtypeStruct((t, IDX_HEADS), F32)],
        compiler_params=_params("parallel"))(misc, gq, gkv, wq, wk, wv, *tab_m, *tab_i)


def _indexer_kernel(q_ref, w_ref, k_ref, bias_ref, sc_ref, *, tq, tk, topk, max_iter):
    i = pl.program_id(1)
    nk = bias_ref.shape[2]
    nb = (i * tq + tq - 1) // tk + 1
    kf = float(topk)
    qpos = i * tq + lax.broadcasted_iota(jnp.int32, (1, tq), 1)
    kpos0 = lax.broadcasted_iota(jnp.int32, (tk, 1), 0)
    nt = (((1,), (1,)), ((), ()))
    groups = tk // SUBLANES
    fold_rows = 4 * SUBLANES

    def fold_sum(x):
        return jnp.sum(x.reshape(tk // fold_rows, fold_rows, tq), axis=0)

    def score_blk(j, carry):
        mx, mn = carry
        start = pl.multiple_of(j * tk, tk)
        k_lo = k_ref[pl.ds(start, tk), 0:LANES]
        k_hi = k_ref[pl.ds(start, tk), LANES:2 * LANES]
        acc = jnp.zeros((tk, tq), F32)
        for p in range(IDX_HEADS // 2):
            qp = q_ref[:, p * LANES:(p + 1) * LANES]
            s0 = lax.dot_general(k_lo, qp, nt, preferred_element_type=F32)
            s1 = lax.dot_general(k_hi, qp, nt, preferred_element_type=F32)
            acc = acc + jnp.maximum(s0, 0.0) * w_ref[2 * p:2 * p + 1, :]
            acc = acc + jnp.maximum(s1, 0.0) * w_ref[2 * p + 1:2 * p + 2, :]
        causal = (j * tk + kpos0) <= qpos
        acc = acc + 0.0
        sc_ref[j] = jnp.where(causal, acc, -jnp.inf)
        mx = jnp.maximum(mx, jnp.max(jnp.where(causal, acc, -jnp.inf).reshape(groups, SUBLANES, tq), axis=0))
        mn = jnp.minimum(mn, jnp.min(jnp.where(causal, acc, jnp.inf).reshape(groups, SUBLANES, tq), axis=0))
        return mx, mn

    mx8, mn8 = lax.fori_loop(0, nb, score_blk, (jnp.full((SUBLANES, tq), -jnp.inf, F32),
                                                 jnp.full((SUBLANES, tq), jnp.inf, F32)))
    rmax = jnp.max(mx8, axis=0, keepdims=True)
    rmin = jnp.min(mn8, axis=0, keepdims=True)

    def count_ge(th):
        def body(j, c):
            return c + fold_sum(jnp.where(sc_ref[j] >= th, 1.0, 0.0))
        part = lax.fori_loop(0, nb, body, jnp.zeros((fold_rows, tq), F32))
        return jnp.sum(part, axis=0, keepdims=True)

    c_all = count_ge(rmin)
    c_top = count_ge(rmax)
    top_full = c_top >= kf
    lo = jnp.where(top_full, rmax, rmin)
    c_lo = jnp.where(top_full, c_top, c_all)
    hi = jnp.where(top_full, jnp.inf, rmax)
    c_hi = jnp.where(top_full, 0.0, c_top)
    done = jnp.where(top_full | (c_lo <= kf), 1.0, 0.0)

    def cond(st):
        it, _, _, _, _, dn = st
        return jnp.logical_and(it < max_iter, jnp.min(dn) < 0.5)

    def step(st):
        it, lo, hi, c_lo, c_hi, dn = st
        th = 0.5 * lo + 0.5 * hi
        stuck = (th <= lo) | (th >= hi)
        c = count_ge(th)
        active = jnp.logical_not(stuck) & (dn < 0.5)
        up = (c >= kf) & active
        down = (c < kf) & active
        lo = jnp.where(up, th, lo)
        c_lo = jnp.where(up, c, c_lo)
        hi = jnp.where(down, th, hi)
        c_hi = jnp.where(down, c, c_hi)
        dn = jnp.where(stuck | (c_lo <= kf), 1.0, dn)
        return it + 1, lo, hi, c_lo, c_hi, dn

    _, lo, hi, c_lo, c_hi, _ = lax.while_loop(cond, step, (jnp.int32(0), lo, hi, c_lo, c_hi, done))

    need = kf - c_hi
    last_key = jnp.full((1, tq), nk * tk - 1, jnp.int32)

    def band_prefix(kcut):
        def body(j, c):
            s = sc_ref[j]
            hit = (s >= lo) & (s < hi) & ((j * tk + kpos0) < kcut)
            return c + fold_sum(jnp.where(hit, 1.0, 0.0))
        part = lax.fori_loop(0, nb, body, jnp.zeros((fold_rows, tq), F32))
        return jnp.sum(part, axis=0, keepdims=True)

    def tie_search(_):
        nbits = int(nk * tk - 1).bit_length()
        kcut = jnp.zeros((1, tq), jnp.int32)
        for b in range(nbits - 1, -1, -1):
            trial = kcut | (1 << b)
            kcut = jnp.where(band_prefix(trial) < need, trial, kcut)
        return kcut

    has_excess = jnp.max(c_lo) > kf
    kcut = lax.cond(has_excess, tie_search, lambda _: last_key, 0)
    kcut = jnp.where(c_lo > kf, kcut, last_key)

    def emit(j, carry):
        s = sc_ref[j]
        sel = (s >= lo) & ((s >= hi) | ((j * tk + kpos0) <= kcut))
        bias_ref[0, 0, j] = jnp.where(sel, 0.0, MASK_NEG).T.astype(bias_ref.dtype)
        return carry

    lax.fori_loop(0, nb, emit, 0)

    def fill(j, carry):
        bias_ref[0, 0, j] = jnp.full((tq, tk), MASK_NEG, bias_ref.dtype)
        return carry

    lax.fori_loop(nb, nk, fill, 0)


def _indexer(qi, wi_t, ki2, batch, seq, topk, tq=ATTN_TQ, tk=ATTN_TK):
    nq, nk = seq // tq, seq // tk
    kern = functools.partial(_indexer_kernel, tq=tq, tk=tk, topk=topk, max_iter=128)
    return pl.pallas_call(
        kern, grid=(batch, nq),
        in_specs=[pl.BlockSpec((tq, IDX_HEADS * IDX_DIM), lambda b, i: (b * nq + i, 0)),
                  pl.BlockSpec((IDX_HEADS, tq), lambda b, i: (0, b * nq + i)),
                  pl.BlockSpec((seq, 2 * LANES), lambda b, i: (b, 0))],
        out_specs=pl.BlockSpec((1, 1, nk, tq, tk), lambda b, i: (b, i, 0, 0, 0)),
        out_shape=jax.ShapeDtypeStruct((batch, nq, nk, tq, tk), BF16),
        scratch_shapes=[pltpu.VMEM((nk, tk, tq), F32)],
        compiler_params=_params("parallel", "arbitrary"))(qi, wi_t, ki2)


def _attn_kernel(it_ref, jt_ref, fl_ref, q_ref, k_ref, v_ref, *rest, heads, dqk, dv, use_bias):
    if use_bias:
        bias_ref, o_ref, m_ref, l_ref, acc_ref = rest
    else:
        o_ref, m_ref, l_ref, acc_ref = rest
    step = pl.program_id(1)
    i = it_ref[step]
    j = jt_ref[step]
    flags = fl_ref[step]
    tq = q_ref.shape[0]
    tk = k_ref.shape[0]
    nch = tk // LANES
    nt = (((1,), (1,)), ((), ()))

    @pl.when(j == 0)
    def _():
        m_ref[...] = jnp.full(m_ref.shape, -jnp.inf, F32)
        l_ref[...] = jnp.zeros(l_ref.shape, F32)
        acc_ref[...] = jnp.zeros(acc_ref.shape, F32)

    def body(masked):
        if use_bias:
            bias = bias_ref[0, 0, 0].astype(F32)
        elif masked:
            row = i * tq + lax.broadcasted_iota(jnp.int32, (tq, 1), 0)
            col = j * tk + lax.broadcasted_iota(jnp.int32, (1, tk), 1)
            causal = col <= row
        for h in range(heads):
            q = q_ref[:, h * dqk:(h + 1) * dqk]
            k = k_ref[:, h * dqk:(h + 1) * dqk]
            v = v_ref[:, h * dv:(h + 1) * dv]
            s = lax.dot_general(q, k, nt, preferred_element_type=F32)
            if use_bias:
                s = s + bias
            elif masked:
                s = jnp.where(causal, s, MASK_NEG)
            sc = [s[:, c * LANES:(c + 1) * LANES] for c in range(nch)]
            mc = sc[0]
            for c in range(1, nch):
                mc = jnp.maximum(mc, sc[c])
            m_prev = m_ref[h]
            m_new = jnp.maximum(m_prev, jnp.max(mc, axis=1, keepdims=True))
            alpha = jnp.exp2(m_prev - m_new)
            ps = [jnp.exp2(sc[c] - m_new) for c in range(nch)]
            lsum = ps[0]
            for c in range(1, nch):
                lsum = lsum + ps[c]
            l_ref[h] = alpha * l_ref[h] + jnp.sum(lsum, axis=1, keepdims=True)
            p = jnp.concatenate(ps, axis=1).astype(BF16)
            pv = jnp.dot(p, v, preferred_element_type=F32)
            acc_ref[:, h * dv:(h + 1) * dv] = alpha * acc_ref[:, h * dv:(h + 1) * dv] + pv
            m_ref[h] = m_new

    if use_bias:
        body(False)
    else:
        diag = (flags & 2) != 0
        pl.when(diag)(lambda: body(True))
        pl.when(jnp.logical_not(diag))(lambda: body(False))

    @pl.when((flags & 1) != 0)
    def _():
        for h in range(heads):
            inv = 1.0 / l_ref[h]
            o_ref[:, h * dv:(h + 1) * dv] = (acc_ref[:, h * dv:(h + 1) * dv] * inv).astype(o_ref.dtype)


def _attention(q, k, v, bias, batch, seq, heads, dqk, dv, q_col=0, k_col=0, tq=ATTN_TQ, tk=ATTN_TK):
    assert dv == LANES
    nq, nk = seq // tq, seq // tk
    pairs = [(i, j) for i in range(nq) for j in range((i * tq + tq - 1) // tk + 1)]
    it = jnp.asarray([p[0] for p in pairs], jnp.int32)
    jt = jnp.asarray([p[1] for p in pairs], jnp.int32)
    fl = jnp.asarray([(1 if j == (i * tq + tq - 1) // tk else 0) + (2 if (j + 1) * tk - 1 > i * tq else 0)
                      for i, j in pairs], jnp.int32)
    use_bias = bias is not None
    in_specs = [pl.BlockSpec((tq, heads * dqk), lambda b, s, it, jt, fl: (b * nq + it[s], q_col)),
                pl.BlockSpec((tk, heads * dqk), lambda b, s, it, jt, fl: (b * nk + jt[s], k_col)),
                pl.BlockSpec((tk, heads * dv), lambda b, s, it, jt, fl: (b * nk + jt[s], 0))]
    args = [q, k, v]
    if use_bias:
        in_specs.append(pl.BlockSpec((1, 1, 1, tq, tk), lambda b, s, it, jt, fl: (b, it[s], jt[s], 0, 0)))
        args.append(bias)
    kern = functools.partial(_attn_kernel, heads=heads, dqk=dqk, dv=dv, use_bias=use_bias)
    return pl.pallas_call(
        kern,
        grid_spec=pltpu.PrefetchScalarGridSpec(
            num_scalar_prefetch=3, grid=(batch, len(pairs)), in_specs=in_specs,
            out_specs=pl.BlockSpec((tq, heads * dv), lambda b, s, it, jt, fl: (b * nq + it[s], 0)),
            scratch_shapes=[pltpu.VMEM((heads, tq, LANES), F32), pltpu.VMEM((heads, tq, LANES), F32),
                            pltpu.VMEM((tq, heads * dv), F32)]),
        out_shape=jax.ShapeDtypeStruct((batch * seq, heads * dv), BF16),
        compiler_params=_params("parallel", "arbitrary"))(it, jt, fl, *args)


def _residual_ln(x, upd, g, b, alpha):
    y = alpha * x + upd
    mu = jnp.mean(y, axis=-1, keepdims=True)
    d = y - mu
    var = jnp.mean(jnp.square(d), axis=-1, keepdims=True)
    return d * lax.rsqrt(var + LN_EPS) * g + b


def _outproj_ln_kernel(a_ref, b_ref, wa_ref, wb_ref, x_ref, g_ref, beta_ref, o_ref, *, alpha):
    mix = jnp.dot(a_ref[...], wa_ref[...], preferred_element_type=F32)
    mix = mix + jnp.dot(b_ref[...], wb_ref[...], preferred_element_type=F32)
    o_ref[...] = _residual_ln(x_ref[...], mix, g_ref[...], beta_ref[...], alpha)


def _outproj_ln(a, b, wa, wb, x, g, beta, alpha, tm=512):
    t, d = x.shape
    row = lambda w: pl.BlockSpec((tm, w), lambda i: (i, 0))
    full = lambda arr: pl.BlockSpec(arr.shape, lambda i: (0, 0))
    return pl.pallas_call(
        functools.partial(_outproj_ln_kernel, alpha=alpha), grid=(t // tm,),
        in_specs=[row(a.shape[1]), row(b.shape[1]), full(wa), full(wb), row(d), full(g), full(beta)],
        out_specs=row(d), out_shape=jax.ShapeDtypeStruct((t, d), F32),
        compiler_params=_params("parallel"))(a, b, wa, wb, x, g, beta)


def _ffn_up_kernel(x_ref, wg_ref, wu_ref, h_ref):
    xb = x_ref[...].astype(BF16)
    g = jnp.dot(xb, wg_ref[...], preferred_element_type=F32)
    u = jnp.dot(xb, wu_ref[...], preferred_element_type=F32)
    h_ref[...] = (g * (1.0 / (1.0 + jnp.exp(-g))) * u).astype(h_ref.dtype)


def _ffn_up(x, wg, wu, tm=1024, tn=512):
    t, d = x.shape
    f = wg.shape[1]
    return pl.pallas_call(
        _ffn_up_kernel, grid=(t // tm, f // tn),
        in_specs=[pl.BlockSpec((tm, d), lambda i, j: (i, 0)),
                  pl.BlockSpec((d, tn), lambda i, j: (0, j)),
                  pl.BlockSpec((d, tn), lambda i, j: (0, j))],
        out_specs=pl.BlockSpec((tm, tn), lambda i, j: (i, j)),
        out_shape=jax.ShapeDtypeStruct((t, f), BF16),
        compiler_params=_params("parallel", "arbitrary"))(x, wg, wu)


def _ffn_down_ln_kernel(h_ref, w_ref, x_ref, g_ref, beta_ref, o_ref, acc_ref, *, alpha):
    k = pl.program_id(1)

    @pl.when(k == 0)
    def _():
        acc_ref[...] = jnp.zeros(acc_ref.shape, F32)

    acc_ref[...] += jnp.dot(h_ref[...], w_ref[...], preferred_element_type=F32)

    @pl.when(k == pl.num_programs(1) - 1)
    def _():
        o_ref[...] = _residual_ln(x_ref[...], acc_ref[...], g_ref[...], beta_ref[...], alpha)


def _ffn_down_ln(h, w, x, g, beta, alpha, tm=512, tk=1408):
    t, d = x.shape
    f = h.shape[1]
    return pl.pallas_call(
        functools.partial(_ffn_down_ln_kernel, alpha=alpha), grid=(t // tm, f // tk),
        in_specs=[pl.BlockSpec((tm, tk), lambda i, k: (i, k)),
                  pl.BlockSpec((tk, d), lambda i, k: (k, 0)),
                  pl.BlockSpec((tm, d), lambda i, k: (i, 0)),
                  pl.BlockSpec((1, d), lambda i, k: (0, 0)),
                  pl.BlockSpec((1, d), lambda i, k: (0, 0))],
        out_specs=pl.BlockSpec((tm, d), lambda i, k: (i, 0)),
        out_shape=jax.ShapeDtypeStruct((t, d), F32),
        scratch_shapes=[pltpu.VMEM((tm, d), F32)],
        compiler_params=_params("parallel", "arbitrary"))(h, w, x, g, beta)


def _pack_mixer_weights(w_in, w_uq, w_ukv):
    sl = lambda k: w_in[:, _OFF[k]:_OFF[k + 1]]
    qa, ka, va, qi, wi, ki, cq, ckv, kr = (sl(k) for k in range(9))
    d = w_in.shape[0]
    zeros = lambda n: jnp.zeros((d, n), w_in.dtype)
    w_misc = jnp.concatenate([cq, ckv, kr, wi, zeros(LANES - MLA_ROPE - IDX_HEADS), ki, zeros(LANES - IDX_DIM)], axis=1)
    w_qk = jnp.concatenate([qa, ka], axis=1)
    uq = w_uq.reshape(Q_LORA, MLA_HEADS, MLA_NOPE + MLA_ROPE)
    uq = jnp.pad(uq, ((0, 0), (0, 0), (0, MLA_QK_PAD - MLA_NOPE - MLA_ROPE))).reshape(Q_LORA, MLA_HEADS * MLA_QK_PAD)
    ukv = w_ukv.reshape(KV_LORA, MLA_HEADS, MLA_NOPE + MLA_V)
    uk = ukv[:, :, :MLA_NOPE].reshape(KV_LORA, MLA_HEADS * MLA_NOPE)
    uv = ukv[:, :, MLA_NOPE:].reshape(KV_LORA, MLA_HEADS * MLA_V)
    return tuple(a.astype(BF16) for a in (w_qk, va, qi, w_misc, uq, uk, uv))


def kernel(x, positions, w_in, g_cq, g_ckv, w_uq, w_ukv, w_o, ln1_g, ln1_b, w_gate, w_up, w_down, ln2_g, ln2_b):
    batch, seq, d = x.shape
    depth = w_in.shape[0]
    alpha = (2 * depth) ** 0.25
    topk = min(TOPK_MAX, seq // 4)
    tab_a = _rope_tables(positions, A_ROT_DIM, ROPE_THETA, A_HEAD_DIM)
    tab_i = _rope_tables(positions, IDX_ROT_DIM, ROPE_THETA, IDX_DIM)
    tab_m = _rope_tables(positions, MLA_ROPE, MLA_ROPE_THETA, LANES)
    xt = x.reshape(batch * seq, d)
    for l in range(depth):
        w_qk, w_v, w_qi, w_misc, uq, uk, uv = _pack_mixer_weights(w_in[l], w_uq[l], w_ukv[l])
        qk = _project(xt, w_qk, BF16, tab_a, A_ROT_DIM // 2, scale0=A_QSCALE)
        va = _project(xt, w_v, BF16)
        qi = _project(xt, w_qi, BF16, tab_i, IDX_ROT_DIM // 2)
        misc = _project(xt, w_misc, F32)
        q_mla, k_mla, v_mla, ki2, wi = _mla_prep(
            misc, g_cq[l].reshape(1, -1), g_ckv[l].reshape(1, -1), uq, uk, uv, tab_m, tab_i)
        bias = _indexer(qi, wi.T, ki2, batch, seq, topk)
        out_a = _attention(qk, qk, va, bias, batch, seq, A_HEADS, A_HEAD_DIM, A_HEAD_DIM, q_col=0, k_col=1)
        out_b = _attention(q_mla, k_mla, v_mla, None, batch, seq, MLA_HEADS, MLA_QK_PAD, MLA_V)
        wo = w_o[l].astype(BF16)
        xt = _outproj_ln(out_a, out_b, wo[:A_WIDTH], wo[A_WIDTH:], xt,
                         ln1_g[l].reshape(1, -1), ln1_b[l].reshape(1, -1), alpha)
        h = _ffn_up(xt, w_gate[l].astype(BF16), w_up[l].astype(BF16))
        xt = _ffn_down_ln(h, w_down[l].astype(BF16), xt,
                          ln2_g[l].reshape(1, -1), ln2_b[l].reshape(1, -1), alpha)
    return xt.reshape(batch, seq, d)
```

```python
import functools
import math

import numpy as np
import jax
import jax.numpy as jnp
from jax import lax
from jax.experimental import pallas as pl
from jax.experimental.pallas import tpu as pltpu

F32 = jnp.float32
BF16 = jnp.bfloat16

D_MODEL = 2048
A_HEADS = 8
A_HEAD_DIM = 128
A_ROT_DIM = A_HEAD_DIM // 4
IDX_HEADS = 16
IDX_DIM = 64
IDX_ROT_DIM = IDX_DIM // 4
TOPK_MAX = 256
MLA_HEADS = 8
MLA_NOPE = 128
MLA_ROPE = 64
MLA_V = 128
MLA_QK_PAD = 256
Q_LORA = 512
KV_LORA = 256
FFN_DIM = 5632
ROPE_THETA = 500000.0
MLA_ROPE_THETA = 10000.0
LN_EPS = 1e-5
RMS_EPS = 1e-6
A_WIDTH = A_HEADS * A_HEAD_DIM
LANES = 128
SUBLANES = 8
MASK_NEG = -1e30
VMEM_LIMIT = 56 * 1024 * 1024
LOG2E = math.log2(math.e)
A_QSCALE = A_HEAD_DIM ** -0.5 * LOG2E
MLA_QSCALE = (MLA_NOPE + MLA_ROPE) ** -0.5 * LOG2E
ATTN_TQ = 512
ATTN_TK = 512
ROW_CHUNK = 32
CHUNK_UNROLL = 2

_OFF = np.cumsum([0, A_WIDTH, A_WIDTH, A_WIDTH, IDX_HEADS * IDX_DIM, IDX_HEADS, IDX_DIM,
                  Q_LORA, KV_LORA, MLA_ROPE]).tolist()
MISC_W = 1024
MISC_CQ = 0
MISC_CKV = Q_LORA
MISC_KR = Q_LORA + KV_LORA
MISC_KI = MISC_KR + LANES


def _params(*sem):
    return pltpu.CompilerParams(dimension_semantics=sem, vmem_limit_bytes=VMEM_LIMIT)


def _rope_tables(positions, rot_dim, theta, period):
    half = rot_dim // 2
    inv_freq = theta ** (-2.0 * jnp.arange(half, dtype=F32) / rot_dim)
    ang = positions.astype(F32)[..., None] * inv_freq
    cos = jnp.cos(ang).reshape(-1, half)
    sin = jnp.sin(ang).reshape(-1, half)
    t = cos.shape[0]
    rest = period - rot_dim
    z_h = jnp.zeros((t, half), F32)
    z_r = jnp.zeros((t, rest), F32)
    c = jnp.concatenate([cos, cos, jnp.ones((t, rest), F32)], axis=1)
    s1 = jnp.concatenate([-sin, z_h, z_r], axis=1)
    s2 = jnp.concatenate([z_h, sin, z_r], axis=1)
    rep = LANES // period
    return tuple(jnp.tile(a, (1, rep)) for a in (c, s1, s2))


def _rope_chunk(a, c, s1, s2, half):
    return a * c + pltpu.roll(a, LANES - half, 1) * s1 + pltpu.roll(a, half, 1) * s2


def _store_cols(o_ref, k, val):
    if len(o_ref.shape) == 3:
        o_ref[k] = val.astype(o_ref.dtype)
    else:
        o_ref[:, k * LANES:(k + 1) * LANES] = val.astype(o_ref.dtype)


def _mm_kernel(x_ref, w_ref, o_ref):
    acc = jnp.dot(x_ref[...], w_ref[...], preferred_element_type=F32)
    if len(o_ref.shape) == 3:
        for k in range(acc.shape[1] // LANES):
            _store_cols(o_ref, k, acc[:, k * LANES:(k + 1) * LANES])
    else:
        o_ref[...] = acc.astype(o_ref.dtype)


def _mm_rope_kernel(x_ref, w_ref, c_ref, s1_ref, s2_ref, o_ref, *, half, scale0):
    acc = jnp.dot(x_ref[...], w_ref[...], preferred_element_type=F32)
    if scale0 is not None:
        acc = acc * jnp.where(pl.program_id(1) == 0, scale0, 1.0)
    c, s1, s2 = c_ref[...], s1_ref[...], s2_ref[...]
    for k in range(acc.shape[1] // LANES):
        _store_cols(o_ref, k, _rope_chunk(acc[:, k * LANES:(k + 1) * LANES], c, s1, s2, half))


def _project(x, w, out_dtype, tables=None, half=0, scale0=None, head_major=False, tm=1024, tn=1024):
    t, kdim = x.shape
    n = w.shape[1]
    grid = (t // tm, n // tn)
    x_spec = pl.BlockSpec((tm, kdim), lambda i, j: (i, 0))
    w_spec = pl.BlockSpec((kdim, tn), lambda i, j: (0, j))
    if head_major:
        o_spec = pl.BlockSpec((tn // LANES, tm, LANES), lambda i, j: (j, i, 0))
        o_shape = jax.ShapeDtypeStruct((n // LANES, t, LANES), out_dtype)
    else:
        o_spec = pl.BlockSpec((tm, tn), lambda i, j: (i, j))
        o_shape = jax.ShapeDtypeStruct((t, n), out_dtype)
    if tables is None:
        return pl.pallas_call(
            _mm_kernel, grid=grid, in_specs=[x_spec, w_spec], out_specs=o_spec, out_shape=o_shape,
            compiler_params=_params("parallel", "arbitrary"))(x, w)
    t_spec = pl.BlockSpec((tm, LANES), lambda i, j: (i, 0))
    return pl.pallas_call(
        functools.partial(_mm_rope_kernel, half=half, scale0=scale0), grid=grid,
        in_specs=[x_spec, w_spec, t_spec, t_spec, t_spec], out_specs=o_spec, out_shape=o_shape,
        compiler_params=_params("parallel", "arbitrary"))(x, w, *tables)


def _mla_prep_kernel(misc_ref, gq_ref, gkv_ref, wq_ref, wk_ref, wv_ref,
                     cm_ref, s1m_ref, s2m_ref, ci_ref, s1i_ref, s2i_ref,
                     q_ref, k_ref, v_ref, ki_ref, wi_ref):
    tm = misc_ref.shape[0]
    cm, s1m, s2m = cm_ref[...], s1m_ref[...], s2m_ref[...]
    lane = lax.broadcasted_iota(jnp.int32, (tm, LANES), 1)

    cq = misc_ref[:, MISC_CQ:MISC_CQ + Q_LORA]
    cqn = cq * lax.rsqrt(jnp.mean(jnp.square(cq), axis=-1, keepdims=True) + RMS_EPS) * gq_ref[...]
    q_all = jnp.dot(cqn.astype(BF16), wq_ref[...], preferred_element_type=F32) * MLA_QSCALE
    for h in range(MLA_HEADS):
        base = h * MLA_QK_PAD
        q_ref[h, :, 0:MLA_NOPE] = q_all[:, base:base + MLA_NOPE].astype(BF16)
        pe = q_all[:, base + MLA_NOPE:base + MLA_QK_PAD]
        q_ref[h, :, MLA_NOPE:MLA_QK_PAD] = _rope_chunk(pe, cm, s1m, s2m, MLA_ROPE // 2).astype(BF16)

    ckv = misc_ref[:, MISC_CKV:MISC_CKV + KV_LORA]
    ckvn = (ckv * lax.rsqrt(jnp.mean(jnp.square(ckv), axis=-1, keepdims=True) + RMS_EPS) * gkv_ref[...]).astype(BF16)
    kn = jnp.dot(ckvn, wk_ref[...], preferred_element_type=F32)
    vn = jnp.dot(ckvn, wv_ref[...], preferred_element_type=F32)

    krc = misc_ref[:, MISC_KR:MISC_KR + LANES]
    kpe = jnp.where(lane < MLA_ROPE, _rope_chunk(krc, cm, s1m, s2m, MLA_ROPE // 2), 0.0).astype(BF16)
    for h in range(MLA_HEADS):
        k_ref[h, :, 0:MLA_NOPE] = kn[:, h * MLA_NOPE:(h + 1) * MLA_NOPE].astype(BF16)
        k_ref[h, :, MLA_NOPE:MLA_QK_PAD] = kpe
        v_ref[h] = vn[:, h * MLA_V:(h + 1) * MLA_V].astype(BF16)

    kic = misc_ref[:, MISC_KI:MISC_KI + LANES]
    ki_lo = _rope_chunk(kic, ci_ref[...], s1i_ref[...], s2i_ref[...], IDX_ROT_DIM // 2)
    ki_ref[:, 0:LANES] = ki_lo.astype(BF16)
    ki_ref[:, LANES:2 * LANES] = pltpu.roll(ki_lo, IDX_DIM, 1).astype(BF16)
    wi_ref[...] = krc[:, MLA_ROPE:MLA_ROPE + IDX_HEADS] * (IDX_HEADS ** -0.5 * IDX_DIM ** -0.5)


def _mla_prep(misc, gq, gkv, wq, wk, wv, tab_m, tab_i, tm=256):
    t = misc.shape[0]
    row = lambda w: pl.BlockSpec((tm, w), lambda i: (i, 0))
    full = lambda a: pl.BlockSpec(a.shape, lambda i: (0, 0))
    heads = lambda w: pl.BlockSpec((MLA_HEADS, tm, w), lambda i: (0, i, 0))
    return pl.pallas_call(
        _mla_prep_kernel, grid=(t // tm,),
        in_specs=[row(MISC_W), full(gq), full(gkv), full(wq), full(wk), full(wv)] + [row(LANES)] * 6,
        out_specs=[heads(MLA_QK_PAD), heads(MLA_QK_PAD), heads(MLA_V), row(2 * LANES), row(IDX_HEADS)],
        out_shape=[jax.ShapeDtypeStruct((MLA_HEADS, t, MLA_QK_PAD), BF16),
                   jax.ShapeDtypeStruct((MLA_HEADS, t, MLA_QK_PAD), BF16),
                   jax.ShapeDtypeStruct((MLA_HEADS, t, MLA_V), BF16),
                   jax.ShapeDtypeStruct((t, 2 * LANES), BF16),
                   jax.ShapeDtypeStruct((t, IDX_HEADS), F32)],
        compiler_params=_params("parallel"))(misc, gq, gkv, wq, wk, wv, *tab_m, *tab_i)


def _indexer_kernel(q_ref, w_ref, k_ref, bias_ref, sc_ref, *, tq, tk, topk, max_iter):
    i = pl.program_id(1)
    nk = bias_ref.shape[2]
    nb = (i * tq + tq - 1) // tk + 1
    kf = float(topk)
    qpos = i * tq + lax.broadcasted_iota(jnp.int32, (1, tq), 1)
    kpos0 = lax.broadcasted_iota(jnp.int32, (tk, 1), 0)
    nt = (((1,), (1,)), ((), ()))
    groups = tk // SUBLANES
    fold_rows = 4 * SUBLANES

    def fold_sum(x):
        return jnp.sum(x.reshape(tk // fold_rows, fold_rows, tq), axis=0)

    def score_blk(j, carry):
        mx, mn = carry
        start = pl.multiple_of(j * tk, tk)
        k_lo = k_ref[pl.ds(start, tk), 0:LANES]
        k_hi = k_ref[pl.ds(start, tk), LANES:2 * LANES]
        acc = jnp.zeros((tk, tq), F32)
        for p in range(IDX_HEADS // 2):
            qp = q_ref[:, p * LANES:(p + 1) * LANES]
            s0 = lax.dot_general(k_lo, qp, nt, preferred_element_type=F32)
            s1 = lax.dot_general(k_hi, qp, nt, preferred_element_type=F32)
            acc = acc + jnp.maximum(s0, 0.0) * w_ref[2 * p:2 * p + 1, :]
            acc = acc + jnp.maximum(s1, 0.0) * w_ref[2 * p + 1:2 * p + 2, :]
        causal = (j * tk + kpos0) <= qpos
        acc = acc + 0.0
        sc_ref[j] = jnp.where(causal, acc, -jnp.inf)
        mx = jnp.maximum(mx, jnp.max(jnp.where(causal, acc, -jnp.inf).reshape(groups, SUBLANES, tq), axis=0))
        mn = jnp.minimum(mn, jnp.min(jnp.where(causal, acc, jnp.inf).reshape(groups, SUBLANES, tq), axis=0))
        return mx, mn

    mx8, mn8 = lax.fori_loop(0, nb, score_blk, (jnp.full((SUBLANES, tq), -jnp.inf, F32),
                                                 jnp.full((SUBLANES, tq), jnp.inf, F32)))
    rmax = jnp.max(mx8, axis=0, keepdims=True)
    rmin = jnp.min(mn8, axis=0, keepdims=True)

    def count_ge(th):
        def body(j, c):
            return c + fold_sum(jnp.where(sc_ref[j] >= th, 1.0, 0.0))
        part = lax.fori_loop(0, nb, body, jnp.zeros((fold_rows, tq), F32))
        return jnp.sum(part, axis=0, keepdims=True)

    c_all = (qpos + 1).astype(F32)
    c_top = count_ge(rmax)
    top_full = c_top >= kf
    lo = jnp.where(top_full, rmax, rmin)
    c_lo = jnp.where(top_full, c_top, c_all)
    hi = jnp.where(top_full, jnp.inf, rmax)
    c_hi = jnp.where(top_full, 0.0, c_top)
    done = jnp.where(top_full | (c_lo <= kf), 1.0, 0.0)

    def cond(st):
        it, _, _, _, _, dn = st
        return jnp.logical_and(it < max_iter, jnp.min(dn) < 0.5)

    def step(st):
        it, lo, hi, c_lo, c_hi, dn = st
        th = 0.5 * lo + 0.5 * hi
        stuck = (th <= lo) | (th >= hi)
        c = count_ge(th)
        active = jnp.logical_not(stuck) & (dn < 0.5)
        up = (c >= kf) & active
        down = (c < kf) & active
        lo = jnp.where(up, th, lo)
        c_lo = jnp.where(up, c, c_lo)
        hi = jnp.where(down, th, hi)
        c_hi = jnp.where(down, c, c_hi)
        dn = jnp.where(stuck | (c_lo <= kf), 1.0, dn)
        return it + 1, lo, hi, c_lo, c_hi, dn

    _, lo, hi, c_lo, c_hi, _ = lax.while_loop(cond, step, (jnp.int32(0), lo, hi, c_lo, c_hi, done))

    need = kf - c_hi
    last_key = jnp.full((1, tq), nk * tk - 1, jnp.int32)

    def band_prefix(kcut):
        def body(j, c):
            s = sc_ref[j]
            hit = (s >= lo) & (s < hi) & ((j * tk + kpos0) < kcut)
            return c + fold_sum(jnp.where(hit, 1.0, 0.0))
        part = lax.fori_loop(0, nb, body, jnp.zeros((fold_rows, tq), F32))
        return jnp.sum(part, axis=0, keepdims=True)

    def tie_search(_):
        nbits = int(nk * tk - 1).bit_length()
        kcut = jnp.zeros((1, tq), jnp.int32)
        for b in range(nbits - 1, -1, -1):
            trial = kcut | (1 << b)
            kcut = jnp.where(band_prefix(trial) < need, trial, kcut)
        return kcut

    has_excess = jnp.max(c_lo) > kf
    kcut = lax.cond(has_excess, tie_search, lambda _: last_key, 0)
    kcut = jnp.where(c_lo > kf, kcut, last_key)

    def emit(j, carry):
        s = sc_ref[j]
        sel = (s >= lo) & ((s >= hi) | ((j * tk + kpos0) <= kcut))
        bias_ref[0, 0, j] = jnp.where(sel, 0.0, MASK_NEG).T.astype(bias_ref.dtype)
        return carry

    lax.fori_loop(0, nb, emit, 0)

    def fill(j, carry):
        bias_ref[0, 0, j] = jnp.full((tq, tk), MASK_NEG, bias_ref.dtype)
        return carry

    lax.fori_loop(nb, nk, fill, 0)


def _indexer(qi, wi_t, ki2, batch, seq, topk, tq=ATTN_TQ, tk=ATTN_TK):
    nq, nk = seq // tq, seq // tk
    kern = functools.partial(_indexer_kernel, tq=tq, tk=tk, topk=topk, max_iter=128)
    return pl.pallas_call(
        kern, grid=(batch, nq),
        in_specs=[pl.BlockSpec((tq, IDX_HEADS * IDX_DIM), lambda b, i: (b * nq + i, 0)),
                  pl.BlockSpec((IDX_HEADS, tq), lambda b, i: (0, b * nq + i)),
                  pl.BlockSpec((seq, 2 * LANES), lambda b, i: (b, 0))],
        out_specs=pl.BlockSpec((1, 1, nk, tq, tk), lambda b, i: (b, i, 0, 0, 0)),
        out_shape=jax.ShapeDtypeStruct((batch, nq, nk, tq, tk), BF16),
        scratch_shapes=[pltpu.VMEM((nk, tk, tq), F32)],
        compiler_params=_params("parallel", "arbitrary"))(qi, wi_t, ki2)


def _attn_kernel(it_ref, jt_ref, fl_ref, q_ref, k_ref, v_ref, *rest, heads, use_bias):
    if use_bias:
        bias_ref, o_ref, m_ref, l_ref, al_ref, acc_ref, s_ref, p_ref, bf_ref = rest
    else:
        o_ref, m_ref, l_ref, al_ref, acc_ref, s_ref, p_ref, bf_ref = rest
    step = pl.program_id(1)
    i = it_ref[step]
    j = jt_ref[step]
    flags = fl_ref[step]
    tq = q_ref.shape[1]
    tk = k_ref.shape[1]
    nch = tk // LANES
    nt = (((1,), (1,)), ((), ()))

    @pl.when(j == 0)
    def _():
        m_ref[...] = jnp.full(m_ref.shape, -jnp.inf, F32)
        l_ref[...] = jnp.zeros(l_ref.shape, F32)
        acc_ref[...] = jnp.zeros(acc_ref.shape, F32)

    def logits(h, slot, biased):
        s = lax.dot_general(q_ref[h], k_ref[h], nt, preferred_element_type=F32)
        if biased:
            s = s + bf_ref[...]
        s_ref[slot] = s

    def softmax(h, slot):
        mc = s_ref[slot, :, 0:LANES]
        for c in range(1, nch):
            mc = jnp.maximum(mc, s_ref[slot, :, c * LANES:(c + 1) * LANES])
        m_prev = m_ref[h]
        m_new = jnp.maximum(m_prev, jnp.max(mc, axis=1, keepdims=True))
        alpha = jnp.exp2(m_prev - m_new)
        m_ref[h] = m_new
        al_ref[h] = alpha
        lparts = []
        for r in range(tq // ROW_CHUNK):
            rows = slice(r * ROW_CHUNK, (r + 1) * ROW_CHUNK)
            m_rows = m_new[rows]
            lsum = None
            for c in range(nch):
                cols = slice(c * LANES, (c + 1) * LANES)
                pc = jnp.exp2(s_ref[slot, rows, cols] - m_rows)
                lsum = pc if lsum is None else lsum + pc
                p_ref[slot, rows, cols] = pc.astype(BF16)
            lparts.append(lsum)
        lsum_all = jnp.concatenate(lparts, axis=0)
        l_ref[h] = alpha * l_ref[h] + jnp.sum(lsum_all, axis=1, keepdims=True)

    def weighted_values(h, slot):
        pv = jnp.dot(p_ref[slot], v_ref[h], preferred_element_type=F32)
        acc_ref[h] = al_ref[h] * acc_ref[h] + pv

    def body(biased):
        logits(0, 0, biased)
        logits(1, 1, biased)
        softmax(0, 0)

        for h in range(1, heads - 1):
            cur = h % 2
            logits(h + 1, 1 - cur, biased)
            softmax(h, cur)
            weighted_values(h - 1, 1 - cur)
        last = (heads - 1) % 2
        softmax(heads - 1, last)
        weighted_values(heads - 2, 1 - last)
        weighted_values(heads - 1, last)

    if use_bias:
        bf_ref[...] = bias_ref[0, 0, 0].astype(F32)
        body(True)
    else:
        diag = (flags & 2) != 0

        @pl.when(diag)
        def _():
            row = i * tq + lax.broadcasted_iota(jnp.int32, (tq, 1), 0)
            col = j * tk + lax.broadcasted_iota(jnp.int32, (1, tk), 1)
            bf_ref[...] = jnp.where(col <= row, 0.0, MASK_NEG)
            body(True)

        pl.when(jnp.logical_not(diag))(lambda: body(False))

    @pl.when((flags & 1) != 0)
    def _():
        for h in range(heads):
            o_ref[:, h * LANES:(h + 1) * LANES] = (acc_ref[h] * (1.0 / l_ref[h])).astype(o_ref.dtype)


def _attention(q, k, v, bias, batch, seq, q_head0=0, k_head0=0, heads=8, tq=ATTN_TQ, tk=ATTN_TK):
    dqk, dv = q.shape[2], v.shape[2]
    assert dv == LANES and q_head0 % heads == 0 and k_head0 % heads == 0
    qb, kb = q_head0 // heads, k_head0 // heads
    nq, nk = seq // tq, seq // tk
    pairs = [(i, j) for i in range(nq) for j in range((i * tq + tq - 1) // tk + 1)]
    it = jnp.asarray([p[0] for p in pairs], jnp.int32)
    jt = jnp.asarray([p[1] for p in pairs], jnp.int32)
    fl = jnp.asarray([(1 if j == (i * tq + tq - 1) // tk else 0) + (2 if (j + 1) * tk - 1 > i * tq else 0)
                      for i, j in pairs], jnp.int32)
    use_bias = bias is not None
    in_specs = [pl.BlockSpec((heads, tq, dqk), lambda b, s, it, jt, fl: (qb, b * nq + it[s], 0)),
                pl.BlockSpec((heads, tk, dqk), lambda b, s, it, jt, fl: (kb, b * nk + jt[s], 0)),
                pl.BlockSpec((heads, tk, dv), lambda b, s, it, jt, fl: (0, b * nk + jt[s], 0))]
    args = [q, k, v]
    if use_bias:
        in_specs.append(pl.BlockSpec((1, 1, 1, tq, tk), lambda b, s, it, jt, fl: (b, it[s], jt[s], 0, 0)))
        args.append(bias)
    stat = pltpu.VMEM((heads, tq, LANES), F32)
    scratch = [stat, stat, stat, stat, pltpu.VMEM((2, tq, tk), F32), pltpu.VMEM((2, tq, tk), BF16),
               pltpu.VMEM((tq, tk), F32)]
    return pl.pallas_call(
        functools.partial(_attn_kernel, heads=heads, use_bias=use_bias),
        grid_spec=pltpu.PrefetchScalarGridSpec(
            num_scalar_prefetch=3, grid=(batch, len(pairs)), in_specs=in_specs,
            out_specs=pl.BlockSpec((tq, heads * dv), lambda b, s, it, jt, fl: (b * nq + it[s], 0)),
            scratch_shapes=scratch),
        out_shape=jax.ShapeDtypeStruct((batch * seq, heads * dv), BF16),
        compiler_params=_params("parallel", "arbitrary"))(it, jt, fl, *args)


def _residual_ln(x, upd, g, b, alpha):
    y = alpha * x + upd
    mu = jnp.mean(y, axis=-1, keepdims=True)
    d = y - mu
    var = jnp.mean(jnp.square(d), axis=-1, keepdims=True)
    return d * lax.rsqrt(var + LN_EPS) * g + b


def _outproj_ln_kernel(a_ref, b_ref, wa_ref, wb_ref, x_ref, g_ref, beta_ref, o_ref, ob_ref, *, alpha):
    mix = jnp.dot(a_ref[...], wa_ref[...], preferred_element_type=F32)
    mix = mix + jnp.dot(b_ref[...], wb_ref[...], preferred_element_type=F32)
    y = _residual_ln(x_ref[...], mix, g_ref[...], beta_ref[...], alpha)
    o_ref[...] = y
    ob_ref[...] = y.astype(BF16)


def _outproj_ln(a, b, wa, wb, x, g, beta, alpha, tm=512):
    t, d = x.shape
    row = lambda w: pl.BlockSpec((tm, w), lambda i: (i, 0))
    full = lambda arr: pl.BlockSpec(arr.shape, lambda i: (0, 0))
    return pl.pallas_call(
        functools.partial(_outproj_ln_kernel, alpha=alpha), grid=(t // tm,),
        in_specs=[row(a.shape[1]), row(b.shape[1]), full(wa), full(wb), row(d), full(g), full(beta)],
        out_specs=[row(d), row(d)],
        out_shape=[jax.ShapeDtypeStruct((t, d), F32), jax.ShapeDtypeStruct((t, d), BF16)],
        compiler_params=_params("parallel"))(a, b, wa, wb, x, g, beta)


def _ffn_up_kernel(x_ref, wg_ref, wu_ref, h_ref):
    xb = x_ref[...]
    g = jnp.dot(xb, wg_ref[...], preferred_element_type=F32)
    u = jnp.dot(xb, wu_ref[...], preferred_element_type=F32)
    h_ref[...] = (g * (1.0 / (1.0 + jnp.exp(-g))) * u).astype(h_ref.dtype)


def _ffn_up(x, wg, wu, tm=2048, tn=512):
    t, d = x.shape
    f = wg.shape[1]
    return pl.pallas_call(
        _ffn_up_kernel, grid=(t // tm, f // tn),
        in_specs=[pl.BlockSpec((tm, d), lambda i, j: (i, 0)),
                  pl.BlockSpec((d, tn), lambda i, j: (0, j)),
                  pl.BlockSpec((d, tn), lambda i, j: (0, j))],
        out_specs=pl.BlockSpec((tm, tn), lambda i, j: (i, j)),
        out_shape=jax.ShapeDtypeStruct((t, f), BF16),
        compiler_params=_params("parallel", "arbitrary"))(x, wg, wu)


def _ffn_down_ln_kernel(h_ref, w_ref, x_ref, g_ref, beta_ref, o_ref, ob_ref, acc_ref, *, alpha):
    k = pl.program_id(1)

    @pl.when(k == 0)
    def _():
        acc_ref[...] = jnp.zeros(acc_ref.shape, F32)

    acc_ref[...] += jnp.dot(h_ref[...], w_ref[...], preferred_element_type=F32)

    @pl.when(k == pl.num_programs(1) - 1)
    def _():
        y = _residual_ln(x_ref[...], acc_ref[...], g_ref[...], beta_ref[...], alpha)
        o_ref[...] = y
        ob_ref[...] = y.astype(BF16)


def _ffn_down_ln(h, w, x, g, beta, alpha, tm=512, tk=1408):
    t, d = x.shape
    f = h.shape[1]
    return pl.pallas_call(
        functools.partial(_ffn_down_ln_kernel, alpha=alpha), grid=(t // tm, f // tk),
        in_specs=[pl.BlockSpec((tm, tk), lambda i, k: (i, k)),
                  pl.BlockSpec((tk, d), lambda i, k: (k, 0)),
                  pl.BlockSpec((tm, d), lambda i, k: (i, 0)),
                  pl.BlockSpec((1, d), lambda i, k: (0, 0)),
                  pl.BlockSpec((1, d), lambda i, k: (0, 0))],
        out_specs=[pl.BlockSpec((tm, d), lambda i, k: (i, 0)), pl.BlockSpec((tm, d), lambda i, k: (i, 0))],
        out_shape=[jax.ShapeDtypeStruct((t, d), F32), jax.ShapeDtypeStruct((t, d), BF16)],
        scratch_shapes=[pltpu.VMEM((tm, d), F32)],
        compiler_params=_params("parallel", "arbitrary"))(h, w, x, g, beta)


def _pack_mixer_weights(w_in, w_uq, w_ukv):
    sl = lambda k: w_in[:, _OFF[k]:_OFF[k + 1]]
    qa, ka, va, qi, wi, ki, cq, ckv, kr = (sl(k) for k in range(9))
    d = w_in.shape[0]
    zeros = lambda n: jnp.zeros((d, n), w_in.dtype)
    w_misc = jnp.concatenate([cq, ckv, kr, wi, zeros(LANES - MLA_ROPE - IDX_HEADS), ki, zeros(LANES - IDX_DIM)], axis=1)
    w_qk = jnp.concatenate([qa, ka], axis=1)
    uq = w_uq.reshape(Q_LORA, MLA_HEADS, MLA_NOPE + MLA_ROPE)
    uq = jnp.pad(uq, ((0, 0), (0, 0), (0, MLA_QK_PAD - MLA_NOPE - MLA_ROPE))).reshape(Q_LORA, MLA_HEADS * MLA_QK_PAD)
    ukv = w_ukv.reshape(KV_LORA, MLA_HEADS, MLA_NOPE + MLA_V)
    uk = ukv[:, :, :MLA_NOPE].reshape(KV_LORA, MLA_HEADS * MLA_NOPE)
    uv = ukv[:, :, MLA_NOPE:].reshape(KV_LORA, MLA_HEADS * MLA_V)
    return tuple(a.astype(BF16) for a in (w_qk, va, qi, w_misc, uq, uk, uv))


def kernel(x, positions, w_in, g_cq, g_ckv, w_uq, w_ukv, w_o, ln1_g, ln1_b, w_gate, w_up, w_down, ln2_g, ln2_b):
    batch, seq, d = x.shape
    depth = w_in.shape[0]
    alpha = (2 * depth) ** 0.25
    topk = min(TOPK_MAX, seq // 4)
    tab_a = _rope_tables(positions, A_ROT_DIM, ROPE_THETA, A_HEAD_DIM)
    tab_i = _rope_tables(positions, IDX_ROT_DIM, ROPE_THETA, IDX_DIM)
    tab_m = _rope_tables(positions, MLA_ROPE, MLA_ROPE_THETA, LANES)
    xt = x.reshape(batch * seq, d)
    xb = xt.astype(BF16)
    for l in range(depth):
        w_qk, w_v, w_qi, w_misc, uq, uk, uv = _pack_mixer_weights(w_in[l], w_uq[l], w_ukv[l])
        qk = _project(xb, w_qk, BF16, tab_a, A_ROT_DIM // 2, scale0=A_QSCALE, head_major=True)
        va = _project(xb, w_v, BF16, head_major=True)
        qi = _project(xb, w_qi, BF16, tab_i, IDX_ROT_DIM // 2)
        misc = _project(xb, w_misc, F32)
        q_mla, k_mla, v_mla, ki2, wi = _mla_prep(
            misc, g_cq[l].reshape(1, -1), g_ckv[l].reshape(1, -1), uq, uk, uv, tab_m, tab_i)
        bias = _indexer(qi, wi.T, ki2, batch, seq, topk)
        out_a = _attention(qk, qk, va, bias, batch, seq, q_head0=0, k_head0=A_HEADS, heads=A_HEADS)
        out_b = _attention(q_mla, k_mla, v_mla, None, batch, seq, heads=MLA_HEADS)
        wo = w_o[l].astype(BF16)
        xt, xb = _outproj_ln(out_a, out_b, wo[:A_WIDTH], wo[A_WIDTH:], xt,
                             ln1_g[l].reshape(1, -1), ln1_b[l].reshape(1, -1), alpha)
        h = _ffn_up(xb, w_gate[l].astype(BF16), w_up[l].astype(BF16))
        xt, xb = _ffn_down_ln(h, w_down[l].astype(BF16), xt,
                              ln2_g[l].reshape(1, -1), ln2_b[l].reshape(1, -1), alpha)
    return xt.reshape(batch, seq, d)
```

```python
import functools
import math

import numpy as np
import jax
import jax.numpy as jnp
from jax import lax
from jax.experimental import pallas as pl
from jax.experimental.pallas import tpu as pltpu

F32 = jnp.float32
BF16 = jnp.bfloat16

D_MODEL = 2048
A_HEADS = 8
A_HEAD_DIM = 128
A_ROT_DIM = A_HEAD_DIM // 4
IDX_HEADS = 16
IDX_DIM = 64
IDX_ROT_DIM = IDX_DIM // 4
TOPK_MAX = 256
MLA_HEADS = 8
MLA_NOPE = 128
MLA_ROPE = 64
MLA_V = 128
MLA_QK_PAD = 256
Q_LORA = 512
KV_LORA = 256
FFN_DIM = 5632
ROPE_THETA = 500000.0
MLA_ROPE_THETA = 10000.0
LN_EPS = 1e-5
RMS_EPS = 1e-6
A_WIDTH = A_HEADS * A_HEAD_DIM
LANES = 128
SUBLANES = 8
MASK_NEG = -1e30
VMEM_LIMIT = 56 * 1024 * 1024
LOG2E = math.log2(math.e)
A_QSCALE = A_HEAD_DIM ** -0.5 * LOG2E
MLA_QSCALE = (MLA_NOPE + MLA_ROPE) ** -0.5 * LOG2E
ATTN_TQ = 512
ATTN_TK = 512
FFN_DOWN_COLS = 1024
LN_ROWS = 256

_OFF = np.cumsum([0, A_WIDTH, A_WIDTH, A_WIDTH, IDX_HEADS * IDX_DIM, IDX_HEADS, IDX_DIM,
                  Q_LORA, KV_LORA, MLA_ROPE]).tolist()
MISC_W = 1024
MISC_CQ = 0
MISC_CKV = Q_LORA
MISC_KR = Q_LORA + KV_LORA
MISC_KI = MISC_KR + LANES


def _params(*sem):
    return pltpu.CompilerParams(dimension_semantics=sem, vmem_limit_bytes=VMEM_LIMIT)


def _rope_tables(positions, rot_dim, theta, period):
    half = rot_dim // 2
    inv_freq = theta ** (-2.0 * jnp.arange(half, dtype=F32) / rot_dim)
    ang = positions.astype(F32)[..., None] * inv_freq
    cos = jnp.cos(ang).reshape(-1, half)
    sin = jnp.sin(ang).reshape(-1, half)
    t = cos.shape[0]
    rest = period - rot_dim
    z_h = jnp.zeros((t, half), F32)
    z_r = jnp.zeros((t, rest), F32)
    c = jnp.concatenate([cos, cos, jnp.ones((t, rest), F32)], axis=1)
    s1 = jnp.concatenate([-sin, z_h, z_r], axis=1)
    s2 = jnp.concatenate([z_h, sin, z_r], axis=1)
    rep = LANES // period
    return tuple(jnp.tile(a, (1, rep)) for a in (c, s1, s2))


def _rope_chunk(a, c, s1, s2, half):
    return a * c + pltpu.roll(a, LANES - half, 1) * s1 + pltpu.roll(a, half, 1) * s2


def _store_cols(o_ref, k, val):
    if len(o_ref.shape) == 3:
        o_ref[k] = val.astype(o_ref.dtype)
    else:
        o_ref[:, k * LANES:(k + 1) * LANES] = val.astype(o_ref.dtype)


def _mm_kernel(x_ref, w_ref, o_ref):
    acc = jnp.dot(x_ref[...], w_ref[...], preferred_element_type=F32)
    if len(o_ref.shape) == 3:
        for k in range(acc.shape[1] // LANES):
            _store_cols(o_ref, k, acc[:, k * LANES:(k + 1) * LANES])
    else:
        o_ref[...] = acc.astype(o_ref.dtype)


def _mm_rope_kernel(x_ref, w_ref, c_ref, s1_ref, s2_ref, o_ref, *, half, scale0):
    acc = jnp.dot(x_ref[...], w_ref[...], preferred_element_type=F32)
    if scale0 is not None:
        acc = acc * jnp.where(pl.program_id(1) == 0, scale0, 1.0)
    c, s1, s2 = c_ref[...], s1_ref[...], s2_ref[...]
    for k in range(acc.shape[1] // LANES):
        _store_cols(o_ref, k, _rope_chunk(acc[:, k * LANES:(k + 1) * LANES], c, s1, s2, half))


def _project(x, w, out_dtype, tables=None, half=0, scale0=None, head_major=False, layer=0, col0=0, n=None,
             tm=1024, tn=1024):
    t, kdim = x.shape
    n = w.shape[-1] if n is None else n
    grid = (t // tm, n // tn)
    x_spec = pl.BlockSpec((tm, kdim), lambda i, j: (i, 0))
    if w.ndim == 3:
        w_spec = pl.BlockSpec((None, kdim, tn), lambda i, j: (layer, 0, col0 + j))
    else:
        w_spec = pl.BlockSpec((kdim, tn), lambda i, j: (0, j))
    if head_major:
        o_spec = pl.BlockSpec((tn // LANES, tm, LANES), lambda i, j: (j, i, 0))
        o_shape = jax.ShapeDtypeStruct((n // LANES, t, LANES), out_dtype)
    else:
        o_spec = pl.BlockSpec((tm, tn), lambda i, j: (i, j))
        o_shape = jax.ShapeDtypeStruct((t, n), out_dtype)
    if tables is None:
        return pl.pallas_call(
            _mm_kernel, grid=grid, in_specs=[x_spec, w_spec], out_specs=o_spec, out_shape=o_shape,
            compiler_params=_params("parallel", "arbitrary"))(x, w)
    t_spec = pl.BlockSpec((tm, LANES), lambda i, j: (i, 0))
    return pl.pallas_call(
        functools.partial(_mm_rope_kernel, half=half, scale0=scale0), grid=grid,
        in_specs=[x_spec, w_spec, t_spec, t_spec, t_spec], out_specs=o_spec, out_shape=o_shape,
        compiler_params=_params("parallel", "arbitrary"))(x, w, *tables)


def _mla_prep_kernel(misc_ref, gq_ref, gkv_ref, wq_ref, wk_ref, wv_ref,
                     cm_ref, s1m_ref, s2m_ref, ci_ref, s1i_ref, s2i_ref,
                     q_ref, k_ref, v_ref, ki_ref, wi_ref):
    tm = misc_ref.shape[0]
    cm, s1m, s2m = cm_ref[...], s1m_ref[...], s2m_ref[...]
    lane = lax.broadcasted_iota(jnp.int32, (tm, LANES), 1)

    cq = misc_ref[:, MISC_CQ:MISC_CQ + Q_LORA]
    cqn = cq * lax.rsqrt(jnp.mean(jnp.square(cq), axis=-1, keepdims=True) + RMS_EPS) * gq_ref[...]
    q_all = jnp.dot(cqn.astype(BF16), wq_ref[...], preferred_element_type=F32) * MLA_QSCALE
    for h in range(MLA_HEADS):
        base = h * MLA_QK_PAD
        q_ref[h, :, 0:MLA_NOPE] = q_all[:, base:base + MLA_NOPE].astype(BF16)
        pe = q_all[:, base + MLA_NOPE:base + MLA_QK_PAD]
        q_ref[h, :, MLA_NOPE:MLA_QK_PAD] = _rope_chunk(pe, cm, s1m, s2m, MLA_ROPE // 2).astype(BF16)

    ckv = misc_ref[:, MISC_CKV:MISC_CKV + KV_LORA]
    ckvn = (ckv * lax.rsqrt(jnp.mean(jnp.square(ckv), axis=-1, keepdims=True) + RMS_EPS) * gkv_ref[...]).astype(BF16)
    kn = jnp.dot(ckvn, wk_ref[...], preferred_element_type=F32)
    vn = jnp.dot(ckvn, wv_ref[...], preferred_element_type=F32)

    krc = misc_ref[:, MISC_KR:MISC_KR + LANES]
    kpe = jnp.where(lane < MLA_ROPE, _rope_chunk(krc, cm, s1m, s2m, MLA_ROPE // 2), 0.0).astype(BF16)
    for h in range(MLA_HEADS):
        k_ref[h, :, 0:MLA_NOPE] = kn[:, h * MLA_NOPE:(h + 1) * MLA_NOPE].astype(BF16)
        k_ref[h, :, MLA_NOPE:MLA_QK_PAD] = kpe
        v_ref[h] = vn[:, h * MLA_V:(h + 1) * MLA_V].astype(BF16)

    kic = misc_ref[:, MISC_KI:MISC_KI + LANES]
    ki_lo = _rope_chunk(kic, ci_ref[...], s1i_ref[...], s2i_ref[...], IDX_ROT_DIM // 2)
    ki_ref[:, 0:LANES] = ki_lo.astype(BF16)
    ki_ref[:, LANES:2 * LANES] = pltpu.roll(ki_lo, IDX_DIM, 1).astype(BF16)
    wi_ref[...] = krc[:, MLA_ROPE:MLA_ROPE + IDX_HEADS] * (IDX_HEADS ** -0.5 * IDX_DIM ** -0.5)


def _mla_prep(misc, gq, gkv, wq, wk, wv, tab_m, tab_i, tm=256):
    t = misc.shape[0]
    row = lambda w: pl.BlockSpec((tm, w), lambda i: (i, 0))
    full = lambda a: pl.BlockSpec(a.shape, lambda i: (0, 0))
    heads = lambda w: pl.BlockSpec((MLA_HEADS, tm, w), lambda i: (0, i, 0))
    return pl.pallas_call(
        _mla_prep_kernel, grid=(t // tm,),
        in_specs=[row(MISC_W), full(gq), full(gkv), full(wq), full(wk), full(wv)] + [row(LANES)] * 6,
        out_specs=[heads(MLA_QK_PAD), heads(MLA_QK_PAD), heads(MLA_V), row(2 * LANES), row(IDX_HEADS)],
        out_shape=[jax.ShapeDtypeStruct((MLA_HEADS, t, MLA_QK_PAD), BF16),
                   jax.ShapeDtypeStruct((MLA_HEADS, t, MLA_QK_PAD), BF16),
                   jax.ShapeDtypeStruct((MLA_HEADS, t, MLA_V), BF16),
                   jax.ShapeDtypeStruct((t, 2 * LANES), BF16),
                   jax.ShapeDtypeStruct((t, IDX_HEADS), F32)],
        compiler_params=_params("parallel"))(misc, gq, gkv, wq, wk, wv, *tab_m, *tab_i)


def _indexer_kernel(q_ref, w_ref, k_ref, bias_ref, sc_ref, *, tq, tk, topk, max_iter):
    i = pl.program_id(1)
    nk = bias_ref.shape[2]
    nb = (i * tq + tq - 1) // tk + 1
    kf = float(topk)
    qpos = i * tq + lax.broadcasted_iota(jnp.int32, (1, tq), 1)
    kpos0 = lax.broadcasted_iota(jnp.int32, (tk, 1), 0)
    nt = (((1,), (1,)), ((), ()))
    groups = tk // SUBLANES
    fold_rows = 4 * SUBLANES

    def fold_sum(x):
        return jnp.sum(x.reshape(tk // fold_rows, fold_rows, tq), axis=0)

    def score_blk(j, carry):
        mx, mn = carry
        start = pl.multiple_of(j * tk, tk)
        k_lo = k_ref[pl.ds(start, tk), 0:LANES]
        k_hi = k_ref[pl.ds(start, tk), LANES:2 * LANES]
        acc = jnp.zeros((tk, tq), F32)
        for p in range(IDX_HEADS // 2):
            qp = q_ref[:, p * LANES:(p + 1) * LANES]
            s0 = lax.dot_general(k_lo, qp, nt, preferred_element_type=F32)
            s1 = lax.dot_general(k_hi, qp, nt, preferred_element_type=F32)
            acc = acc + jnp.maximum(s0, 0.0) * w_ref[2 * p:2 * p + 1, :]
            acc = acc + jnp.maximum(s1, 0.0) * w_ref[2 * p + 1:2 * p + 2, :]
        causal = (j * tk + kpos0) <= qpos
        acc = acc + 0.0
        sc_ref[j] = jnp.where(causal, acc, -jnp.inf)
        mx = jnp.maximum(mx, jnp.max(jnp.where(causal, acc, -jnp.inf).reshape(groups, SUBLANES, tq), axis=0))
        mn = jnp.minimum(mn, jnp.min(jnp.where(causal, acc, jnp.inf).reshape(groups, SUBLANES, tq), axis=0))
        return mx, mn

    mx8, mn8 = lax.fori_loop(0, nb, score_blk, (jnp.full((SUBLANES, tq), -jnp.inf, F32),
                                                 jnp.full((SUBLANES, tq), jnp.inf, F32)))
    rmax = jnp.max(mx8, axis=0, keepdims=True)
    rmin = jnp.min(mn8, axis=0, keepdims=True)

    def count_ge(th):
        def body(j, c):
            return c + fold_sum(jnp.where(sc_ref[j] >= th, 1.0, 0.0))
        part = lax.fori_loop(0, nb, body, jnp.zeros((fold_rows, tq), F32))
        return jnp.sum(part, axis=0, keepdims=True)

    c_all = (qpos + 1).astype(F32)
    c_top = count_ge(rmax)
    top_full = c_top >= kf
    lo = jnp.where(top_full, rmax, rmin)
    c_lo = jnp.where(top_full, c_top, c_all)
    hi = jnp.where(top_full, jnp.inf, rmax)
    c_hi = jnp.where(top_full, 0.0, c_top)
    done = jnp.where(top_full | (c_lo <= kf), 1.0, 0.0)

    def cond(st):
        it, _, _, _, _, dn = st
        return jnp.logical_and(it < max_iter, jnp.min(dn) < 0.5)

    def halve(st):
        it, lo, hi, c_lo, c_hi, dn = st
        th = 0.5 * lo + 0.5 * hi
        stuck = (th <= lo) | (th >= hi)
        c = count_ge(th)
        active = jnp.logical_not(stuck) & (dn < 0.5)
        up = (c >= kf) & active
        down = (c < kf) & active
        lo = jnp.where(up, th, lo)
        c_lo = jnp.where(up, c, c_lo)
        hi = jnp.where(down, th, hi)
        c_hi = jnp.where(down, c, c_hi)
        dn = jnp.where(stuck | (c_lo <= kf), 1.0, dn)
        return it + 1, lo, hi, c_lo, c_hi, dn

    def step(st):
        return halve(halve(st))

    _, lo, hi, c_lo, c_hi, _ = lax.while_loop(cond, step, (jnp.int32(0), lo, hi, c_lo, c_hi, done))

    need = kf - c_hi
    last_key = jnp.full((1, tq), nk * tk - 1, jnp.int32)

    def band_prefix(kcut):
        def body(j, c):
            s = sc_ref[j]
            hit = (s >= lo) & (s < hi) & ((j * tk + kpos0) < kcut)
            return c + fold_sum(jnp.where(hit, 1.0, 0.0))
        part = lax.fori_loop(0, nb, body, jnp.zeros((fold_rows, tq), F32))
        return jnp.sum(part, axis=0, keepdims=True)

    def tie_search(_):
        nbits = int(nk * tk - 1).bit_length()
        kcut = jnp.zeros((1, tq), jnp.int32)
        for b in range(nbits - 1, -1, -1):
            trial = kcut | (1 << b)
            kcut = jnp.where(band_prefix(trial) < need, trial, kcut)
        return kcut

    has_excess = jnp.max(c_lo) > kf
    kcut = lax.cond(has_excess, tie_search, lambda _: last_key, 0)
    kcut = jnp.where(c_lo > kf, kcut, last_key)

    def emit(j, carry):
        s = sc_ref[j]
        sel = (s >= lo) & ((s >= hi) | ((j * tk + kpos0) <= kcut))
        bias_ref[0, 0, j] = jnp.where(sel, 0.0, MASK_NEG).T.astype(bias_ref.dtype)
        return carry

    lax.fori_loop(0, nb, emit, 0)

    def fill(j, carry):
        bias_ref[0, 0, j] = jnp.full((tq, tk), MASK_NEG, bias_ref.dtype)
        return carry

    lax.fori_loop(nb, nk, fill, 0)


def _indexer(qi, wi_t, ki2, batch, seq, topk, tq=ATTN_TQ, tk=ATTN_TK):
    nq, nk = seq // tq, seq // tk
    kern = functools.partial(_indexer_kernel, tq=tq, tk=tk, topk=topk, max_iter=128)
    return pl.pallas_call(
        kern, grid=(batch, nq),
        in_specs=[pl.BlockSpec((tq, IDX_HEADS * IDX_DIM), lambda b, i: (b * nq + i, 0)),
                  pl.BlockSpec((IDX_HEADS, tq), lambda b, i: (0, b * nq + i)),
                  pl.BlockSpec((seq, 2 * LANES), lambda b, i: (b, 0))],
        out_specs=pl.BlockSpec((1, 1, nk, tq, tk), lambda b, i: (b, i, 0, 0, 0)),
        out_shape=jax.ShapeDtypeStruct((batch, nq, nk, tq, tk), BF16),
        scratch_shapes=[pltpu.VMEM((nk, tk, tq), F32)],
        compiler_params=_params("parallel", "arbitrary"))(qi, wi_t, ki2)


def _attn_kernel(it_ref, jt_ref, fl_ref, q_ref, k_ref, v_ref, *rest, heads, use_bias):
    if use_bias:
        bias_ref, o_ref, m_ref, al_ref, acc_ref, s_ref, p_ref, bf_ref, one_ref = rest
    else:
        o_ref, m_ref, al_ref, acc_ref, s_ref, p_ref, bf_ref, one_ref = rest
    step = pl.program_id(1)
    i = it_ref[step]
    j = jt_ref[step]
    flags = fl_ref[step]
    tq = q_ref.shape[1]
    tk = k_ref.shape[1]
    nch = tk // LANES
    nt = (((1,), (1,)), ((), ()))

    @pl.when(j == 0)
    def _():
        m_ref[...] = jnp.full(m_ref.shape, -jnp.inf, F32)
        acc_ref[...] = jnp.zeros(acc_ref.shape, F32)
        one_ref[...] = jnp.ones(one_ref.shape, BF16)

    def logits(h, slot, biased):
        s = lax.dot_general(q_ref[h], k_ref[h], nt, preferred_element_type=F32)
        if biased:
            s = s + bf_ref[...]
        s_ref[slot] = s

    def softmax(h, slot):
        mc = s_ref[slot, :, 0:LANES]
        for c in range(1, nch):
            mc = jnp.maximum(mc, s_ref[slot, :, c * LANES:(c + 1) * LANES])
        m_prev = m_ref[h]
        m_new = jnp.maximum(m_prev, jnp.max(mc, axis=1, keepdims=True))
        alpha = jnp.exp2(m_prev - m_new)
        m_ref[h] = m_new
        al_ref[h] = alpha
        for c in range(nch):
            cols = slice(c * LANES, (c + 1) * LANES)
            p_ref[slot, :, cols] = jnp.exp2(s_ref[slot, :, cols] - m_new).astype(BF16)

    def weighted_values(h, slot):
        v_ext = jnp.concatenate([v_ref[h], one_ref[...]], axis=1)
        pv = jnp.dot(p_ref[slot], v_ext, preferred_element_type=F32)
        alpha = al_ref[h]
        acc_ref[h, :, 0:LANES] = alpha * acc_ref[h, :, 0:LANES] + pv[:, 0:LANES]
        acc_ref[h, :, LANES:2 * LANES] = alpha * acc_ref[h, :, LANES:2 * LANES] + pv[:, LANES:2 * LANES]

    def body(biased):
        logits(0, 0, biased)
        logits(1, 1, biased)
        softmax(0, 0)

        for h in range(1, heads - 1):
            cur = h % 2
            logits(h + 1, 1 - cur, biased)
            softmax(h, cur)
            weighted_values(h - 1, 1 - cur)
        last = (heads - 1) % 2
        softmax(heads - 1, last)
        weighted_values(heads - 2, 1 - last)
        weighted_values(heads - 1, last)

    if use_bias:
        bf_ref[...] = bias_ref[0, 0, 0].astype(F32)
        body(True)
    else:
        diag = (flags & 2) != 0

        @pl.when(diag)
        def _():
            row = i * tq + lax.broadcasted_iota(jnp.int32, (tq, 1), 0)
            col = j * tk + lax.broadcasted_iota(jnp.int32, (1, tk), 1)
            bf_ref[...] = jnp.where(col <= row, 0.0, MASK_NEG)
            body(True)

        pl.when(jnp.logical_not(diag))(lambda: body(False))

    @pl.when((flags & 1) != 0)
    def _():
        for h in range(heads):
            inv = 1.0 / acc_ref[h, :, LANES:2 * LANES]
            o_ref[:, h * LANES:(h + 1) * LANES] = (acc_ref[h, :, 0:LANES] * inv).astype(o_ref.dtype)


def _attention(q, k, v, bias, batch, seq, q_head0=0, k_head0=0, heads=8, tq=ATTN_TQ, tk=ATTN_TK):
    dqk, dv = q.shape[2], v.shape[2]
    assert dv == LANES and q_head0 % heads == 0 and k_head0 % heads == 0
    qb, kb = q_head0 // heads, k_head0 // heads
    nq, nk = seq // tq, seq // tk
    pairs = [(i, j) for i in range(nq) for j in range((i * tq + tq - 1) // tk + 1)]
    it = jnp.asarray([p[0] for p in pairs], jnp.int32)
    jt = jnp.asarray([p[1] for p in pairs], jnp.int32)
    fl = jnp.asarray([(1 if j == (i * tq + tq - 1) // tk else 0) + (2 if (j + 1) * tk - 1 > i * tq else 0)
                      for i, j in pairs], jnp.int32)
    use_bias = bias is not None
    in_specs = [pl.BlockSpec((heads, tq, dqk), lambda b, s, it, jt, fl: (qb, b * nq + it[s], 0)),
                pl.BlockSpec((heads, tk, dqk), lambda b, s, it, jt, fl: (kb, b * nk + jt[s], 0)),
                pl.BlockSpec((heads, tk, dv), lambda b, s, it, jt, fl: (0, b * nk + jt[s], 0))]
    args = [q, k, v]
    if use_bias:
        in_specs.append(pl.BlockSpec((1, 1, 1, tq, tk), lambda b, s, it, jt, fl: (b, it[s], jt[s], 0, 0)))
        args.append(bias)
    stat = pltpu.VMEM((heads, tq, LANES), F32)
    scratch = [stat, stat, pltpu.VMEM((heads, tq, 2 * LANES), F32), pltpu.VMEM((2, tq, tk), F32),
               pltpu.VMEM((2, tq, tk), BF16), pltpu.VMEM((tq, tk), F32), pltpu.VMEM((tk, LANES), BF16)]
    return pl.pallas_call(
        functools.partial(_attn_kernel, heads=heads, use_bias=use_bias),
        grid_spec=pltpu.PrefetchScalarGridSpec(
            num_scalar_prefetch=3, grid=(batch, len(pairs)), in_specs=in_specs,
            out_specs=pl.BlockSpec((tq, heads * dv), lambda b, s, it, jt, fl: (b * nq + it[s], 0)),
            scratch_shapes=scratch),
        out_shape=jax.ShapeDtypeStruct((batch * seq, heads * dv), BF16),
        compiler_params=_params("parallel", "arbitrary"))(it, jt, fl, *args)


def _residual_ln(x, upd, g, b, alpha):
    y = alpha * x + upd
    mu = jnp.mean(y, axis=-1, keepdims=True)
    d = y - mu
    var = jnp.mean(jnp.square(d), axis=-1, keepdims=True)
    return d * lax.rsqrt(var + LN_EPS) * g + b


def _outproj_ln_kernel(a_ref, b_ref, wa_ref, wb_ref, x_ref, g_ref, beta_ref, o_ref, ob_ref, *, alpha):
    mix = jnp.dot(a_ref[...], wa_ref[...], preferred_element_type=F32)
    mix = mix + jnp.dot(b_ref[...], wb_ref[...], preferred_element_type=F32)
    y = _residual_ln(x_ref[...], mix, g_ref[...], beta_ref[...], alpha)
    o_ref[...] = y
    ob_ref[...] = y.astype(BF16)


def _outproj_ln(a, b, w_o, layer, x, g, beta, alpha, tm=512):
    t, d = x.shape
    wa_rows = a.shape[1]
    row = lambda w: pl.BlockSpec((tm, w), lambda i: (i, 0))
    full = lambda arr: pl.BlockSpec(arr.shape, lambda i: (0, 0))
    half = lambda r: pl.BlockSpec((None, wa_rows, d), lambda i: (layer, r, 0))
    return pl.pallas_call(
        functools.partial(_outproj_ln_kernel, alpha=alpha), grid=(t // tm,),
        in_specs=[row(wa_rows), row(b.shape[1]), half(0), half(1), row(d), full(g), full(beta)],
        out_specs=[row(d), row(d)],
        out_shape=[jax.ShapeDtypeStruct((t, d), F32), jax.ShapeDtypeStruct((t, d), BF16)],
        compiler_params=_params("parallel"))(a, b, w_o, w_o, x, g, beta)


def _ffn_up_kernel(x_ref, wg_ref, wu_ref, h_ref):
    xb = x_ref[...]
    g = jnp.dot(xb, wg_ref[...], preferred_element_type=F32)
    u = jnp.dot(xb, wu_ref[...], preferred_element_type=F32)
    h_ref[...] = (g * (1.0 / (1.0 + jnp.exp(-g))) * u).astype(h_ref.dtype)


def _ffn_up(x, wg, wu, layer, tm=2048, tn=512):
    t, d = x.shape
    f = wg.shape[2]
    return pl.pallas_call(
        _ffn_up_kernel, grid=(t // tm, f // tn),
        in_specs=[pl.BlockSpec((tm, d), lambda i, j: (i, 0)),
                  pl.BlockSpec((None, d, tn), lambda i, j: (layer, 0, j)),
                  pl.BlockSpec((None, d, tn), lambda i, j: (layer, 0, j))],
        out_specs=pl.BlockSpec((tm, tn), lambda i, j: (i, j)),
        out_shape=jax.ShapeDtypeStruct((t, f), BF16),
        compiler_params=_params("parallel", "arbitrary"))(x, wg, wu)


def _ffn_down_ln_kernel(h_ref, w_ref, x_ref, g_ref, beta_ref, o_ref, ob_ref, *, alpha):
    k = pl.program_id(1)
    tm, d = o_ref.shape
    for n in range(d // FFN_DOWN_COLS):
        cols = slice(n * FFN_DOWN_COLS, (n + 1) * FFN_DOWN_COLS)
        part = jnp.dot(h_ref[...], w_ref[:, cols], preferred_element_type=F32)

        @pl.when(k == 0)
        def _():
            o_ref[:, cols] = part

        @pl.when(k > 0)
        def _():
            o_ref[:, cols] += part

    @pl.when(k == pl.num_programs(1) - 1)
    def _():
        for r in range(tm // LN_ROWS):
            rows = slice(r * LN_ROWS, (r + 1) * LN_ROWS)
            y = _residual_ln(x_ref[rows, :], o_ref[rows, :], g_ref[...], beta_ref[...], alpha)
            o_ref[rows, :] = y
            ob_ref[rows, :] = y.astype(BF16)


def _ffn_down_ln(h, w, layer, x, g, beta, alpha, tm=1024, tk=512):
    t, d = x.shape
    f = h.shape[1]
    return pl.pallas_call(
        functools.partial(_ffn_down_ln_kernel, alpha=alpha), grid=(t // tm, f // tk),
        in_specs=[pl.BlockSpec((tm, tk), lambda i, k: (i, k)),
                  pl.BlockSpec((None, tk, d), lambda i, k: (layer, k, 0)),
                  pl.BlockSpec((tm, d), lambda i, k: (i, 0)),
                  pl.BlockSpec((1, d), lambda i, k: (0, 0)),
                  pl.BlockSpec((1, d), lambda i, k: (0, 0))],
        out_specs=[pl.BlockSpec((tm, d), lambda i, k: (i, 0)), pl.BlockSpec((tm, d), lambda i, k: (i, 0))],
        out_shape=[jax.ShapeDtypeStruct((t, d), F32), jax.ShapeDtypeStruct((t, d), BF16)],
        compiler_params=_params("parallel", "arbitrary"))(h, w, x, g, beta)


def _pack_small_weights(w_in, w_uq, w_ukv):
    sl = lambda k: w_in[:, _OFF[k]:_OFF[k + 1]]
    wi, ki, cq, ckv, kr = (sl(k) for k in range(4, 9))
    d = w_in.shape[0]
    zeros = lambda n: jnp.zeros((d, n), w_in.dtype)
    w_misc = jnp.concatenate([cq, ckv, kr, wi, zeros(LANES - MLA_ROPE - IDX_HEADS), ki, zeros(LANES - IDX_DIM)], axis=1)
    uq = w_uq.reshape(Q_LORA, MLA_HEADS, MLA_NOPE + MLA_ROPE)
    uq = jnp.pad(uq, ((0, 0), (0, 0), (0, MLA_QK_PAD - MLA_NOPE - MLA_ROPE))).reshape(Q_LORA, MLA_HEADS * MLA_QK_PAD)
    ukv = w_ukv.reshape(KV_LORA, MLA_HEADS, MLA_NOPE + MLA_V)
    uk = ukv[:, :, :MLA_NOPE].reshape(KV_LORA, MLA_HEADS * MLA_NOPE)
    uv = ukv[:, :, MLA_NOPE:].reshape(KV_LORA, MLA_HEADS * MLA_V)
    return w_misc, uq, uk, uv


def kernel(x, positions, w_in, g_cq, g_ckv, w_uq, w_ukv, w_o, ln1_g, ln1_b, w_gate, w_up, w_down, ln2_g, ln2_b):
    batch, seq, d = x.shape
    depth = w_in.shape[0]
    alpha = (2 * depth) ** 0.25
    topk = min(TOPK_MAX, seq // 4)
    tab_a = _rope_tables(positions, A_ROT_DIM, ROPE_THETA, A_HEAD_DIM)
    tab_i = _rope_tables(positions, IDX_ROT_DIM, ROPE_THETA, IDX_DIM)
    tab_m = _rope_tables(positions, MLA_ROPE, MLA_ROPE_THETA, LANES)
    xt = x.reshape(batch * seq, d)
    xb = xt.astype(BF16)
    w_in_b, w_uq_b, w_ukv_b, w_o_b = (w.astype(BF16) for w in (w_in, w_uq, w_ukv, w_o))
    w_gate_b, w_up_b, w_down_b = (w.astype(BF16) for w in (w_gate, w_up, w_down))
    tile = A_WIDTH
    for l in range(depth):
        w_misc, uq, uk, uv = _pack_small_weights(w_in_b[l], w_uq_b[l], w_ukv_b[l])
        qk = _project(xb, w_in_b, BF16, tab_a, A_ROT_DIM // 2, scale0=A_QSCALE, head_major=True,
                      layer=l, col0=0, n=2 * tile, tn=tile)
        va = _project(xb, w_in_b, BF16, head_major=True, layer=l, col0=2, n=tile, tn=tile)
        qi = _project(xb, w_in_b, BF16, tab_i, IDX_ROT_DIM // 2, layer=l, col0=3, n=tile, tn=tile)
        misc = _project(xb, w_misc, F32)
        q_mla, k_mla, v_mla, ki2, wi = _mla_prep(
            misc, g_cq[l].reshape(1, -1), g_ckv[l].reshape(1, -1), uq, uk, uv, tab_m, tab_i)
        bias = _indexer(qi, wi.T, ki2, batch, seq, topk)
        out_a = _attention(qk, qk, va, bias, batch, seq, q_head0=0, k_head0=A_HEADS, heads=A_HEADS)
        out_b = _attention(q_mla, k_mla, v_mla, None, batch, seq, heads=MLA_HEADS)
        xt, xb = _outproj_ln(out_a, out_b, w_o_b, l, xt,
                             ln1_g[l].reshape(1, -1), ln1_b[l].reshape(1, -1), alpha)
        h = _ffn_up(xb, w_gate_b, w_up_b, l)
        xt, xb = _ffn_down_ln(h, w_down_b, l, xt, ln2_g[l].reshape(1, -1), ln2_b[l].reshape(1, -1), alpha)
    return xt.reshape(batch, seq, d)
```

```python
import functools
import math

import numpy as np
import jax
import jax.numpy as jnp
from jax import lax
from jax.experimental import pallas as pl
from jax.experimental.pallas import tpu as pltpu

F32 = jnp.float32
BF16 = jnp.bfloat16

D_MODEL = 2048
A_HEADS = 8
A_HEAD_DIM = 128
A_ROT_DIM = A_HEAD_DIM // 4
IDX_HEADS = 16
IDX_DIM = 64
IDX_ROT_DIM = IDX_DIM // 4
TOPK_MAX = 256
MLA_HEADS = 8
MLA_NOPE = 128
MLA_ROPE = 64
MLA_V = 128
MLA_QK_PAD = 256
Q_LORA = 512
KV_LORA = 256
FFN_DIM = 5632
ROPE_THETA = 500000.0
MLA_ROPE_THETA = 10000.0
LN_EPS = 1e-5
RMS_EPS = 1e-6
A_WIDTH = A_HEADS * A_HEAD_DIM
LANES = 128
SUBLANES = 8
MASK_NEG = -1e30
VMEM_LIMIT = 56 * 1024 * 1024
LOG2E = math.log2(math.e)
A_QSCALE = A_HEAD_DIM ** -0.5 * LOG2E
MLA_QSCALE = (MLA_NOPE + MLA_ROPE) ** -0.5 * LOG2E
ATTN_TQ = 512
ATTN_TK = 512
FFN_DOWN_COLS = 1024
LN_ROWS = 256

_OFF = np.cumsum([0, A_WIDTH, A_WIDTH, A_WIDTH, IDX_HEADS * IDX_DIM, IDX_HEADS, IDX_DIM,
                  Q_LORA, KV_LORA, MLA_ROPE]).tolist()
MISC_W = 1024
MISC_CQ = 0
MISC_CKV = Q_LORA
MISC_KR = Q_LORA + KV_LORA
MISC_KI = MISC_KR + LANES


def _params(*sem):
    return pltpu.CompilerParams(dimension_semantics=sem, vmem_limit_bytes=VMEM_LIMIT)


def _rope_tables(positions, rot_dim, theta, period):
    half = rot_dim // 2
    inv_freq = theta ** (-2.0 * jnp.arange(half, dtype=F32) / rot_dim)
    ang = inv_freq[:, None] * positions.astype(F32).reshape(1, -1)
    cos, sin = lax.optimization_barrier((jnp.cos(ang), jnp.sin(ang)))
    t = cos.shape[1]
    rest = period - rot_dim
    z_h = jnp.zeros((half, t), F32)
    z_r = jnp.zeros((rest, t), F32)
    c = jnp.concatenate([cos, cos, jnp.ones((rest, t), F32)], axis=0)
    s1 = jnp.concatenate([-sin, z_h, z_r], axis=0)
    s2 = jnp.concatenate([z_h, sin, z_r], axis=0)
    rep = LANES // period
    return tuple(jnp.tile(a, (rep, 1)).T for a in (c, s1, s2))


def _rope_chunk(a, c, s1, s2, half):
    return a * c + pltpu.roll(a, LANES - half, 1) * s1 + pltpu.roll(a, half, 1) * s2


def _store_cols(o_ref, k, val):
    if len(o_ref.shape) == 3:
        o_ref[k] = val.astype(o_ref.dtype)
    else:
        o_ref[:, k * LANES:(k + 1) * LANES] = val.astype(o_ref.dtype)


def _mm_kernel(x_ref, w_ref, o_ref):
    acc = jnp.dot(x_ref[...], w_ref[...], preferred_element_type=F32)
    if len(o_ref.shape) == 3:
        for k in range(acc.shape[1] // LANES):
            _store_cols(o_ref, k, acc[:, k * LANES:(k + 1) * LANES])
    else:
        o_ref[...] = acc.astype(o_ref.dtype)


def _mm_rope_kernel(x_ref, w_ref, c_ref, s1_ref, s2_ref, o_ref, *, half, scale0):
    acc = jnp.dot(x_ref[...], w_ref[...], preferred_element_type=F32)
    if scale0 is not None:
        acc = acc * jnp.where(pl.program_id(1) == 0, scale0, 1.0)
    c, s1, s2 = c_ref[...], s1_ref[...], s2_ref[...]
    for k in range(acc.shape[1] // LANES):
        _store_cols(o_ref, k, _rope_chunk(acc[:, k * LANES:(k + 1) * LANES], c, s1, s2, half))


def _project(x, w, out_dtype, tables=None, half=0, scale0=None, head_major=False, layer=0, col0=0, n=None,
             tm=1024, tn=1024):
    t, kdim = x.shape
    n = w.shape[-1] if n is None else n
    grid = (t // tm, n // tn)
    x_spec = pl.BlockSpec((tm, kdim), lambda i, j: (i, 0))
    if w.ndim == 3:
        w_spec = pl.BlockSpec((None, kdim, tn), lambda i, j: (layer, 0, col0 + j))
    else:
        w_spec = pl.BlockSpec((kdim, tn), lambda i, j: (0, j))
    if head_major:
        o_spec = pl.BlockSpec((tn // LANES, tm, LANES), lambda i, j: (j, i, 0))
        o_shape = jax.ShapeDtypeStruct((n // LANES, t, LANES), out_dtype)
    else:
        o_spec = pl.BlockSpec((tm, tn), lambda i, j: (i, j))
        o_shape = jax.ShapeDtypeStruct((t, n), out_dtype)
    if tables is None:
        return pl.pallas_call(
            _mm_kernel, grid=grid, in_specs=[x_spec, w_spec], out_specs=o_spec, out_shape=o_shape,
            compiler_params=_params("parallel", "arbitrary"))(x, w)
    t_spec = pl.BlockSpec((tm, LANES), lambda i, j: (i, 0))
    return pl.pallas_call(
        functools.partial(_mm_rope_kernel, half=half, scale0=scale0), grid=grid,
        in_specs=[x_spec, w_spec, t_spec, t_spec, t_spec], out_specs=o_spec, out_shape=o_shape,
        compiler_params=_params("parallel", "arbitrary"))(x, w, *tables)


def _mla_prep_kernel(misc_ref, gq_ref, gkv_ref, wq_ref, wk_ref, wv_ref,
                     cm_ref, s1m_ref, s2m_ref, ci_ref, s1i_ref, s2i_ref,
                     q_ref, k_ref, v_ref, ki_ref, wi_ref):
    tm = misc_ref.shape[0]
    cm, s1m, s2m = cm_ref[...], s1m_ref[...], s2m_ref[...]
    lane = lax.broadcasted_iota(jnp.int32, (tm, LANES), 1)

    cq = misc_ref[:, MISC_CQ:MISC_CQ + Q_LORA]
    cqn = cq * lax.rsqrt(jnp.mean(jnp.square(cq), axis=-1, keepdims=True) + RMS_EPS) * gq_ref[...]
    q_all = jnp.dot(cqn.astype(BF16), wq_ref[...], preferred_element_type=F32) * MLA_QSCALE
    for h in range(MLA_HEADS):
        base = h * MLA_QK_PAD
        q_ref[h, :, 0:MLA_NOPE] = q_all[:, base:base + MLA_NOPE].astype(BF16)
        pe = q_all[:, base + MLA_NOPE:base + MLA_QK_PAD]
        q_ref[h, :, MLA_NOPE:MLA_QK_PAD] = _rope_chunk(pe, cm, s1m, s2m, MLA_ROPE // 2).astype(BF16)

    ckv = misc_ref[:, MISC_CKV:MISC_CKV + KV_LORA]
    ckvn = (ckv * lax.rsqrt(jnp.mean(jnp.square(ckv), axis=-1, keepdims=True) + RMS_EPS) * gkv_ref[...]).astype(BF16)
    kn = jnp.dot(ckvn, wk_ref[...], preferred_element_type=F32)
    vn = jnp.dot(ckvn, wv_ref[...], preferred_element_type=F32)

    krc = misc_ref[:, MISC_KR:MISC_KR + LANES]
    kpe = jnp.where(lane < MLA_ROPE, _rope_chunk(krc, cm, s1m, s2m, MLA_ROPE // 2), 0.0).astype(BF16)
    for h in range(MLA_HEADS):
        k_ref[h, :, 0:MLA_NOPE] = kn[:, h * MLA_NOPE:(h + 1) * MLA_NOPE].astype(BF16)
        k_ref[h, :, MLA_NOPE:MLA_QK_PAD] = kpe
        v_ref[h] = vn[:, h * MLA_V:(h + 1) * MLA_V].astype(BF16)

    kic = misc_ref[:, MISC_KI:MISC_KI + LANES]
    ki_lo = _rope_chunk(kic, ci_ref[...], s1i_ref[...], s2i_ref[...], IDX_ROT_DIM // 2)
    ki_ref[:, 0:LANES] = ki_lo.astype(BF16)
    ki_ref[:, LANES:2 * LANES] = pltpu.roll(ki_lo, IDX_DIM, 1).astype(BF16)
    wi_ref[...] = krc[:, MLA_ROPE:MLA_ROPE + IDX_HEADS] * (IDX_HEADS ** -0.5 * IDX_DIM ** -0.5)


def _mla_prep(misc, gq, gkv, wq, wk, wv, tab_m, tab_i, tm=512):
    t = misc.shape[0]
    row = lambda w: pl.BlockSpec((tm, w), lambda i: (i, 0))
    full = lambda a: pl.BlockSpec(a.shape, lambda i: (0, 0))
    heads = lambda w: pl.BlockSpec((MLA_HEADS, tm, w), lambda i: (0, i, 0))
    return pl.pallas_call(
        _mla_prep_kernel, grid=(t // tm,),
        in_specs=[row(MISC_W), full(gq), full(gkv), full(wq), full(wk), full(wv)] + [row(LANES)] * 6,
        out_specs=[heads(MLA_QK_PAD), heads(MLA_QK_PAD), heads(MLA_V), row(2 * LANES), row(IDX_HEADS)],
        out_shape=[jax.ShapeDtypeStruct((MLA_HEADS, t, MLA_QK_PAD), BF16),
                   jax.ShapeDtypeStruct((MLA_HEADS, t, MLA_QK_PAD), BF16),
                   jax.ShapeDtypeStruct((MLA_HEADS, t, MLA_V), BF16),
                   jax.ShapeDtypeStruct((t, 2 * LANES), BF16),
                   jax.ShapeDtypeStruct((t, IDX_HEADS), F32)],
        compiler_params=_params("parallel"))(misc, gq, gkv, wq, wk, wv, *tab_m, *tab_i)


def _indexer_kernel(q_ref, w_ref, k_ref, bias_ref, sc_ref, *, tq, tk, topk, max_iter):
    i = pl.program_id(1)
    nk = bias_ref.shape[2]
    nb = (i * tq + tq - 1) // tk + 1
    kf = float(topk)
    qpos = i * tq + lax.broadcasted_iota(jnp.int32, (1, tq), 1)
    kpos0 = lax.broadcasted_iota(jnp.int32, (tk, 1), 0)
    nt = (((1,), (1,)), ((), ()))
    groups = tk // SUBLANES
    fold_rows = 4 * SUBLANES

    def fold_sum(x):
        return jnp.sum(x.reshape(tk // fold_rows, fold_rows, tq), axis=0)

    def score_blk(j, carry):
        mx, mn = carry
        start = pl.multiple_of(j * tk, tk)
        k_lo = k_ref[pl.ds(start, tk), 0:LANES]
        k_hi = k_ref[pl.ds(start, tk), LANES:2 * LANES]
        acc = jnp.zeros((tk, tq), F32)
        for p in range(IDX_HEADS // 2):
            qp = q_ref[:, p * LANES:(p + 1) * LANES]
            s0 = lax.dot_general(k_lo, qp, nt, preferred_element_type=F32)
            s1 = lax.dot_general(k_hi, qp, nt, preferred_element_type=F32)
            acc = acc + jnp.maximum(s0, 0.0) * w_ref[2 * p:2 * p + 1, :]
            acc = acc + jnp.maximum(s1, 0.0) * w_ref[2 * p + 1:2 * p + 2, :]
        causal = (j * tk + kpos0) <= qpos
        acc = acc + 0.0
        sc_ref[j] = jnp.where(causal, acc, -jnp.inf)
        mx = jnp.maximum(mx, jnp.max(jnp.where(causal, acc, -jnp.inf).reshape(groups, SUBLANES, tq), axis=0))
        mn = jnp.minimum(mn, jnp.min(jnp.where(causal, acc, jnp.inf).reshape(groups, SUBLANES, tq), axis=0))
        return mx, mn

    mx8, mn8 = lax.fori_loop(0, nb, score_blk, (jnp.full((SUBLANES, tq), -jnp.inf, F32),
                                                 jnp.full((SUBLANES, tq), jnp.inf, F32)))
    rmax = jnp.max(mx8, axis=0, keepdims=True)
    rmin = jnp.min(mn8, axis=0, keepdims=True)

    def count_ge(th):
        def body(j, c):
            return c + fold_sum(jnp.where(sc_ref[j] >= th, 1.0, 0.0))
        part = lax.fori_loop(0, nb, body, jnp.zeros((fold_rows, tq), F32))
        return jnp.sum(part, axis=0, keepdims=True)

    c_all = (qpos + 1).astype(F32)
    c_top = count_ge(rmax)
    top_full = c_top >= kf
    lo = jnp.where(top_full, rmax, rmin)
    c_lo = jnp.where(top_full, c_top, c_all)
    hi = jnp.where(top_full, jnp.inf, rmax)
    c_hi = jnp.where(top_full, 0.0, c_top)
    done = jnp.where(top_full | (c_lo <= kf), 1.0, 0.0)

    def cond(st):
        it, _, _, _, _, dn = st
        return jnp.logical_and(it < max_iter, jnp.min(dn) < 0.5)

    def halve(st):
        it, lo, hi, c_lo, c_hi, dn = st
        th = 0.5 * lo + 0.5 * hi
        stuck = (th <= lo) | (th >= hi)
        c = count_ge(th)
        active = jnp.logical_not(stuck) & (dn < 0.5)
        up = (c >= kf) & active
        down = (c < kf) & active
        lo = jnp.where(up, th, lo)
        c_lo = jnp.where(up, c, c_lo)
        hi = jnp.where(down, th, hi)
        c_hi = jnp.where(down, c, c_hi)
        dn = jnp.where(stuck | (c_lo <= kf), 1.0, dn)
        return it + 1, lo, hi, c_lo, c_hi, dn

    def step(st):
        return halve(halve(st))

    _, lo, hi, c_lo, c_hi, _ = lax.while_loop(cond, step, (jnp.int32(0), lo, hi, c_lo, c_hi, done))

    need = kf - c_hi
    last_key = jnp.full((1, tq), nk * tk - 1, jnp.int32)

    def band_prefix(kcut):
        def body(j, c):
            s = sc_ref[j]
            hit = (s >= lo) & (s < hi) & ((j * tk + kpos0) < kcut)
            return c + fold_sum(jnp.where(hit, 1.0, 0.0))
        part = lax.fori_loop(0, nb, body, jnp.zeros((fold_rows, tq), F32))
        return jnp.sum(part, axis=0, keepdims=True)

    def tie_search(_):
        nbits = int(nk * tk - 1).bit_length()
        kcut = jnp.zeros((1, tq), jnp.int32)
        for b in range(nbits - 1, -1, -1):
            trial = kcut | (1 << b)
            kcut = jnp.where(band_prefix(trial) < need, trial, kcut)
        return kcut

    has_excess = jnp.max(c_lo) > kf
    kcut = lax.cond(has_excess, tie_search, lambda _: last_key, 0)
    kcut = jnp.where(c_lo > kf, kcut, last_key)

    def emit(j, carry):
        s = sc_ref[j]
        sel = (s >= lo) & ((s >= hi) | ((j * tk + kpos0) <= kcut))
        bias_ref[0, 0, j] = jnp.where(sel, 0.0, MASK_NEG).T.astype(bias_ref.dtype)
        return carry

    lax.fori_loop(0, nb, emit, 0)

    def fill(j, carry):
        bias_ref[0, 0, j] = jnp.full((tq, tk), MASK_NEG, bias_ref.dtype)
        return carry

    lax.fori_loop(nb, nk, fill, 0)


def _indexer(qi, wi_t, ki2, batch, seq, topk, tq=ATTN_TQ, tk=ATTN_TK):
    nq, nk = seq // tq, seq // tk
    kern = functools.partial(_indexer_kernel, tq=tq, tk=tk, topk=topk, max_iter=128)
    return pl.pallas_call(
        kern, grid=(batch, nq),
        in_specs=[pl.BlockSpec((tq, IDX_HEADS * IDX_DIM), lambda b, i: (b * nq + i, 0)),
                  pl.BlockSpec((IDX_HEADS, tq), lambda b, i: (0, b * nq + i)),
                  pl.BlockSpec((seq, 2 * LANES), lambda b, i: (b, 0))],
        out_specs=pl.BlockSpec((1, 1, nk, tq, tk), lambda b, i: (b, i, 0, 0, 0)),
        out_shape=jax.ShapeDtypeStruct((batch, nq, nk, tq, tk), BF16),
        scratch_shapes=[pltpu.VMEM((nk, tk, tq), F32)],
        compiler_params=_params("parallel", "arbitrary"))(qi, wi_t, ki2)


def _attn_kernel(it_ref, jt_ref, fl_ref, q_ref, k_ref, v_ref, *rest, heads, use_bias):
    if use_bias:
        bias_ref, o_ref, m_ref, l_ref, al_ref, acc_ref, s_ref, p_ref, bf_ref = rest
    else:
        o_ref, m_ref, l_ref, al_ref, acc_ref, s_ref, p_ref, bf_ref = rest
    step = pl.program_id(1)
    i = it_ref[step]
    j = jt_ref[step]
    flags = fl_ref[step]
    tq = q_ref.shape[1]
    tk = k_ref.shape[1]
    nch = tk // LANES
    nt = (((1,), (1,)), ((), ()))

    @pl.when(j == 0)
    def _():
        m_ref[...] = jnp.full(m_ref.shape, -jnp.inf, F32)
        l_ref[...] = jnp.zeros(l_ref.shape, F32)
        acc_ref[...] = jnp.zeros(acc_ref.shape, F32)

    def logits(h, slot, biased):
        s = lax.dot_general(q_ref[h], k_ref[h], nt, preferred_element_type=F32)
        if biased:
            s = s + bf_ref[...]
        s_ref[slot] = s

    def softmax(h, slot):
        mc = s_ref[slot, :, 0:LANES]
        for c in range(1, nch):
            mc = jnp.maximum(mc, s_ref[slot, :, c * LANES:(c + 1) * LANES])
        m_prev = m_ref[h]
        m_new = jnp.maximum(m_prev, jnp.max(mc, axis=1, keepdims=True))
        alpha = jnp.exp2(m_prev - m_new)
        m_ref[h] = m_new
        al_ref[h] = alpha
        lsum = None
        for c in range(nch):
            cols = slice(c * LANES, (c + 1) * LANES)
            pc = jnp.exp2(s_ref[slot, :, cols] - m_new)
            lsum = pc if lsum is None else lsum + pc
            p_ref[slot, :, cols] = pc.astype(BF16)
        l_ref[h] = alpha * l_ref[h] + jnp.sum(lsum, axis=1, keepdims=True)

    def weighted_values(h, slot):
        pv = jnp.dot(p_ref[slot], v_ref[h], preferred_element_type=F32)
        acc_ref[h] = al_ref[h] * acc_ref[h] + pv

    def body(biased):
        logits(0, 0, biased)
        logits(1, 1, biased)
        softmax(0, 0)

        for h in range(1, heads - 1):
            cur = h % 2
            logits(h + 1, 1 - cur, biased)
            softmax(h, cur)
            weighted_values(h - 1, 1 - cur)
        last = (heads - 1) % 2
        softmax(heads - 1, last)
        weighted_values(heads - 2, 1 - last)
        weighted_values(heads - 1, last)

    if use_bias:
        bf_ref[...] = bias_ref[0, 0, 0].astype(F32)
        body(True)
    else:
        diag = (flags & 2) != 0

        @pl.when(diag)
        def _():
            row = i * tq + lax.broadcasted_iota(jnp.int32, (tq, 1), 0)
            col = j * tk + lax.broadcasted_iota(jnp.int32, (1, tk), 1)
            bf_ref[...] = jnp.where(col <= row, 0.0, MASK_NEG)
            body(True)

        pl.when(jnp.logical_not(diag))(lambda: body(False))

    @pl.when((flags & 1) != 0)
    def _():
        for h in range(heads):
            o_ref[:, h * LANES:(h + 1) * LANES] = (acc_ref[h] * (1.0 / l_ref[h])).astype(o_ref.dtype)


def _attention(q, k, v, bias, batch, seq, q_head0=0, k_head0=0, heads=8, tq=ATTN_TQ, tk=ATTN_TK):
    dqk, dv = q.shape[2], v.shape[2]
    assert dv == LANES and q_head0 % heads == 0 and k_head0 % heads == 0
    qb, kb = q_head0 // heads, k_head0 // heads
    nq, nk = seq // tq, seq // tk
    pairs = [(i, j) for i in range(nq) for j in range((i * tq + tq - 1) // tk + 1)]
    it = jnp.asarray([p[0] for p in pairs], jnp.int32)
    jt = jnp.asarray([p[1] for p in pairs], jnp.int32)
    fl = jnp.asarray([(1 if j == (i * tq + tq - 1) // tk else 0) + (2 if (j + 1) * tk - 1 > i * tq else 0)
                      for i, j in pairs], jnp.int32)
    use_bias = bias is not None
    in_specs = [pl.BlockSpec((heads, tq, dqk), lambda b, s, it, jt, fl: (qb, b * nq + it[s], 0)),
                pl.BlockSpec((heads, tk, dqk), lambda b, s, it, jt, fl: (kb, b * nk + jt[s], 0)),
                pl.BlockSpec((heads, tk, dv), lambda b, s, it, jt, fl: (0, b * nk + jt[s], 0))]
    args = [q, k, v]
    if use_bias:
        in_specs.append(pl.BlockSpec((1, 1, 1, tq, tk), lambda b, s, it, jt, fl: (b, it[s], jt[s], 0, 0)))
        args.append(bias)
    stat = pltpu.VMEM((heads, tq, LANES), F32)
    scratch = [stat, stat, stat, stat, pltpu.VMEM((2, tq, tk), F32), pltpu.VMEM((2, tq, tk), BF16),
               pltpu.VMEM((tq, tk), F32)]
    return pl.pallas_call(
        functools.partial(_attn_kernel, heads=heads, use_bias=use_bias),
        grid_spec=pltpu.PrefetchScalarGridSpec(
            num_scalar_prefetch=3, grid=(batch, len(pairs)), in_specs=in_specs,
            out_specs=pl.BlockSpec((tq, heads * dv), lambda b, s, it, jt, fl: (b * nq + it[s], 0)),
            scratch_shapes=scratch),
        out_shape=jax.ShapeDtypeStruct((batch * seq, heads * dv), BF16),
        compiler_params=_params("parallel", "arbitrary"))(it, jt, fl, *args)


def _residual_ln(x, upd, g, b, alpha):
    y = alpha * x + upd
    mu = jnp.mean(y, axis=-1, keepdims=True)
    d = y - mu
    var = jnp.mean(jnp.square(d), axis=-1, keepdims=True)
    return d * lax.rsqrt(var + LN_EPS) * g + b


def _outproj_ln_kernel(a_ref, b_ref, wa_ref, wb_ref, x_ref, g_ref, beta_ref, o_ref, ob_ref, *, alpha):
    mix = jnp.dot(a_ref[...], wa_ref[...], preferred_element_type=F32)
    mix = mix + jnp.dot(b_ref[...], wb_ref[...], preferred_element_type=F32)
    y = _residual_ln(x_ref[...], mix, g_ref[...], beta_ref[...], alpha)
    o_ref[...] = y
    ob_ref[...] = y.astype(BF16)


def _outproj_ln(a, b, w_o, layer, x, g, beta, alpha, tm=512):
    t, d = x.shape
    wa_rows = a.shape[1]
    row = lambda w: pl.BlockSpec((tm, w), lambda i: (i, 0))
    full = lambda arr: pl.BlockSpec(arr.shape, lambda i: (0, 0))
    half = lambda r: pl.BlockSpec((None, wa_rows, d), lambda i: (layer, r, 0))
    return pl.pallas_call(
        functools.partial(_outproj_ln_kernel, alpha=alpha), grid=(t // tm,),
        in_specs=[row(wa_rows), row(b.shape[1]), half(0), half(1), row(d), full(g), full(beta)],
        out_specs=[row(d), row(d)],
        out_shape=[jax.ShapeDtypeStruct((t, d), F32), jax.ShapeDtypeStruct((t, d), BF16)],
        compiler_params=_params("parallel"))(a, b, w_o, w_o, x, g, beta)


def _ffn_up_kernel(x_ref, wg_ref, wu_ref, h_ref):
    xb = x_ref[...]
    g = jnp.dot(xb, wg_ref[...], preferred_element_type=F32)
    u = jnp.dot(xb, wu_ref[...], preferred_element_type=F32)
    h_ref[...] = (g * (1.0 / (1.0 + jnp.exp(-g))) * u).astype(h_ref.dtype)


def _ffn_up(x, wg, wu, layer, tm=2048, tn=512):
    t, d = x.shape
    f = wg.shape[2]
    return pl.pallas_call(
        _ffn_up_kernel, grid=(t // tm, f // tn),
        in_specs=[pl.BlockSpec((tm, d), lambda i, j: (i, 0)),
                  pl.BlockSpec((None, d, tn), lambda i, j: (layer, 0, j)),
                  pl.BlockSpec((None, d, tn), lambda i, j: (layer, 0, j))],
        out_specs=pl.BlockSpec((tm, tn), lambda i, j: (i, j)),
        out_shape=jax.ShapeDtypeStruct((t, f), BF16),
        compiler_params=_params("parallel", "arbitrary"))(x, wg, wu)


def _ffn_down_ln_kernel(h_ref, w_ref, x_ref, g_ref, beta_ref, o_ref, *rest, alpha):
    ob_ref, acc_ref = rest if len(rest) == 2 else (None, rest[0])
    k = pl.program_id(1)
    tm, d = o_ref.shape

    @pl.when(k == 0)
    def _():
        acc_ref[...] = jnp.zeros(acc_ref.shape, F32)

    for n in range(d // FFN_DOWN_COLS):
        cols = slice(n * FFN_DOWN_COLS, (n + 1) * FFN_DOWN_COLS)
        acc_ref[:, cols] += jnp.dot(h_ref[...], w_ref[:, cols], preferred_element_type=F32)

    @pl.when(k == pl.num_programs(1) - 1)
    def _():
        for r in range(tm // LN_ROWS):
            rows = slice(r * LN_ROWS, (r + 1) * LN_ROWS)
            y = _residual_ln(x_ref[rows, :], acc_ref[rows, :], g_ref[...], beta_ref[...], alpha)
            o_ref[rows, :] = y
            if ob_ref is not None:
                ob_ref[rows, :] = y.astype(BF16)


def _ffn_down_ln(h, w, layer, x, g, beta, alpha, with_copy, tm=1024, tk=1408):
    t, d = x.shape
    f = h.shape[1]
    once = lambda: pl.BlockSpec((tm, d), lambda i, k: (i, 0), pipeline_mode=pl.Buffered(1))
    return pl.pallas_call(
        functools.partial(_ffn_down_ln_kernel, alpha=alpha), grid=(t // tm, f // tk),
        in_specs=[pl.BlockSpec((tm, tk), lambda i, k: (i, k)),
                  pl.BlockSpec((None, tk, d), lambda i, k: (layer, k, 0)),
                  once(),
                  pl.BlockSpec((1, d), lambda i, k: (0, 0)),
                  pl.BlockSpec((1, d), lambda i, k: (0, 0))],
        out_specs=[once(), once()][:2 if with_copy else 1],
        out_shape=[jax.ShapeDtypeStruct((t, d), F32), jax.ShapeDtypeStruct((t, d), BF16)][:2 if with_copy else 1],
        scratch_shapes=[pltpu.VMEM((tm, d), F32)],
        compiler_params=_params("parallel", "arbitrary"))(h, w, x, g, beta)


def _pack_small_weights(w_in, w_uq, w_ukv):
    sl = lambda k: w_in[:, _OFF[k]:_OFF[k + 1]]
    wi, ki, cq, ckv, kr = (sl(k) for k in range(4, 9))
    d = w_in.shape[0]
    zeros = lambda n: jnp.zeros((d, n), w_in.dtype)
    w_misc = jnp.concatenate([cq, ckv, kr, wi, zeros(LANES - MLA_ROPE - IDX_HEADS), ki, zeros(LANES - IDX_DIM)], axis=1)
    uq = w_uq.reshape(Q_LORA, MLA_HEADS, MLA_NOPE + MLA_ROPE)
    uq = jnp.pad(uq, ((0, 0), (0, 0), (0, MLA_QK_PAD - MLA_NOPE - MLA_ROPE))).reshape(Q_LORA, MLA_HEADS * MLA_QK_PAD)
    ukv = w_ukv.reshape(KV_LORA, MLA_HEADS, MLA_NOPE + MLA_V)
    uk = ukv[:, :, :MLA_NOPE].reshape(KV_LORA, MLA_HEADS * MLA_NOPE)
    uv = ukv[:, :, MLA_NOPE:].reshape(KV_LORA, MLA_HEADS * MLA_V)
    return w_misc, uq, uk, uv


def kernel(x, positions, w_in, g_cq, g_ckv, w_uq, w_ukv, w_o, ln1_g, ln1_b, w_gate, w_up, w_down, ln2_g, ln2_b):
    batch, seq, d = x.shape
    depth = w_in.shape[0]
    alpha = (2 * depth) ** 0.25
    topk = min(TOPK_MAX, seq // 4)
    tab_a = _rope_tables(positions, A_ROT_DIM, ROPE_THETA, A_HEAD_DIM)
    tab_i = _rope_tables(positions, IDX_ROT_DIM, ROPE_THETA, IDX_DIM)
    tab_m = _rope_tables(positions, MLA_ROPE, MLA_ROPE_THETA, LANES)
    xt = x.reshape(batch * seq, d)
    xb = xt.astype(BF16)
    w_in_b, w_uq_b, w_ukv_b, w_o_b = (w.astype(BF16) for w in (w_in, w_uq, w_ukv, w_o))
    w_gate_b, w_up_b, w_down_b = (w.astype(BF16) for w in (w_gate, w_up, w_down))
    tile = A_WIDTH
    for l in range(depth):
        w_misc, uq, uk, uv = _pack_small_weights(w_in_b[l], w_uq_b[l], w_ukv_b[l])
        qk = _project(xb, w_in_b, BF16, tab_a, A_ROT_DIM // 2, scale0=A_QSCALE, head_major=True,
                      layer=l, col0=0, n=2 * tile, tn=tile)
        va = _project(xb, w_in_b, BF16, head_major=True, layer=l, col0=2, n=tile, tn=tile)
        qi = _project(xb, w_in_b, BF16, tab_i, IDX_ROT_DIM // 2, layer=l, col0=3, n=tile, tn=tile)
        misc = _project(xb, w_misc, F32)
        q_mla, k_mla, v_mla, ki2, wi = _mla_prep(
            misc, g_cq[l].reshape(1, -1), g_ckv[l].reshape(1, -1), uq, uk, uv, tab_m, tab_i)
        bias = _indexer(qi, wi.T, ki2, batch, seq, topk)
        out_a = _attention(qk, qk, va, bias, batch, seq, q_head0=0, k_head0=A_HEADS, heads=A_HEADS)
        out_b = _attention(q_mla, k_mla, v_mla, None, batch, seq, heads=MLA_HEADS)
        xt, xb = _outproj_ln(out_a, out_b, w_o_b, l, xt,
                             ln1_g[l].reshape(1, -1), ln1_b[l].reshape(1, -1), alpha)
        h = _ffn_up(xb, w_gate_b, w_up_b, l)
        outs = _ffn_down_ln(h, w_down_b, l, xt, ln2_g[l].reshape(1, -1), ln2_b[l].reshape(1, -1), alpha,
                            with_copy=l + 1 < depth)
        xt, xb = outs if l + 1 < depth else (outs[0], None)
    return xt.reshape(batch, seq, d)
```

```python
import functools
import math

import numpy as np
import jax
import jax.numpy as jnp
from jax import lax
from jax.experimental import pallas as pl
from jax.experimental.pallas import tpu as pltpu

F32 = jnp.float32
BF16 = jnp.bfloat16

D_MODEL = 2048
A_HEADS = 8
A_HEAD_DIM = 128
A_ROT_DIM = A_HEAD_DIM // 4
IDX_HEADS = 16
IDX_DIM = 64
IDX_ROT_DIM = IDX_DIM // 4
TOPK_MAX = 256
MLA_HEADS = 8
MLA_NOPE = 128
MLA_ROPE = 64
MLA_V = 128
MLA_QK_PAD = 256
Q_LORA = 512
KV_LORA = 256
FFN_DIM = 5632
ROPE_THETA = 500000.0
MLA_ROPE_THETA = 10000.0
LN_EPS = 1e-5
RMS_EPS = 1e-6
A_WIDTH = A_HEADS * A_HEAD_DIM
LANES = 128
SUBLANES = 8
MASK_NEG = -1e30
VMEM_LIMIT = 56 * 1024 * 1024
LOG2E = math.log2(math.e)
A_QSCALE = A_HEAD_DIM ** -0.5 * LOG2E
MLA_QSCALE = (MLA_NOPE + MLA_ROPE) ** -0.5 * LOG2E
ATTN_TQ = 512
ATTN_TK = 512

_OFF = np.cumsum([0, A_WIDTH, A_WIDTH, A_WIDTH, IDX_HEADS * IDX_DIM, IDX_HEADS, IDX_DIM,
                  Q_LORA, KV_LORA, MLA_ROPE]).tolist()
MISC_W = 1024
MISC_CQ = 0
MISC_CKV = Q_LORA
MISC_KR = Q_LORA + KV_LORA
MISC_KI = MISC_KR + LANES


def _params(*sem):
    return pltpu.CompilerParams(dimension_semantics=sem, vmem_limit_bytes=VMEM_LIMIT)


def _rope_tables(positions, rot_dim, theta, period):
    half = rot_dim // 2
    inv_freq = theta ** (-2.0 * jnp.arange(half, dtype=F32) / rot_dim)
    ang = inv_freq[:, None] * positions.astype(F32).reshape(1, -1)
    cos, sin = lax.optimization_barrier((jnp.cos(ang), jnp.sin(ang)))
    t = cos.shape[1]
    rest = period - rot_dim
    z_h = jnp.zeros((half, t), F32)
    z_r = jnp.zeros((rest, t), F32)
    c = jnp.concatenate([cos, cos, jnp.ones((rest, t), F32)], axis=0)
    s1 = jnp.concatenate([-sin, z_h, z_r], axis=0)
    s2 = jnp.concatenate([z_h, sin, z_r], axis=0)
    rep = LANES // period
    return tuple(jnp.tile(a, (rep, 1)).T for a in (c, s1, s2))


def _rope_chunk(a, c, s1, s2, half):
    return a * c + pltpu.roll(a, LANES - half, 1) * s1 + pltpu.roll(a, half, 1) * s2


def _store_cols(o_ref, k, val):
    if len(o_ref.shape) == 3:
        o_ref[k] = val.astype(o_ref.dtype)
    else:
        o_ref[:, k * LANES:(k + 1) * LANES] = val.astype(o_ref.dtype)


def _mm_kernel(x_ref, w_ref, o_ref):
    acc = jnp.dot(x_ref[...], w_ref[...], preferred_element_type=F32)
    if len(o_ref.shape) == 3:
        for k in range(acc.shape[1] // LANES):
            _store_cols(o_ref, k, acc[:, k * LANES:(k + 1) * LANES])
    else:
        o_ref[...] = acc.astype(o_ref.dtype)


def _mm_rope_kernel(x_ref, w_ref, c_ref, s1_ref, s2_ref, o_ref, *, half, scale0):
    acc = jnp.dot(x_ref[...], w_ref[...], preferred_element_type=F32)
    if scale0 is not None:
        acc = acc * jnp.where(pl.program_id(1) == 0, scale0, 1.0)
    c, s1, s2 = c_ref[...], s1_ref[...], s2_ref[...]
    for k in range(acc.shape[1] // LANES):
        _store_cols(o_ref, k, _rope_chunk(acc[:, k * LANES:(k + 1) * LANES], c, s1, s2, half))


def _project(x, w, out_dtype, tables=None, half=0, scale0=None, head_major=False, layer=0, col0=0, n=None,
             tm=1024, tn=1024):
    t, kdim = x.shape
    n = w.shape[-1] if n is None else n
    grid = (t // tm, n // tn)
    x_spec = pl.BlockSpec((tm, kdim), lambda i, j: (i, 0))
    if w.ndim == 3:
        w_spec = pl.BlockSpec((None, kdim, tn), lambda i, j: (layer, 0, col0 + j))
    else:
        w_spec = pl.BlockSpec((kdim, tn), lambda i, j: (0, j))
    if head_major:
        o_spec = pl.BlockSpec((tn // LANES, tm, LANES), lambda i, j: (j, i, 0))
        o_shape = jax.ShapeDtypeStruct((n // LANES, t, LANES), out_dtype)
    else:
        o_spec = pl.BlockSpec((tm, tn), lambda i, j: (i, j))
        o_shape = jax.ShapeDtypeStruct((t, n), out_dtype)
    if tables is None:
        return pl.pallas_call(
            _mm_kernel, grid=grid, in_specs=[x_spec, w_spec], out_specs=o_spec, out_shape=o_shape,
            compiler_params=_params("parallel", "arbitrary"))(x, w)
    t_spec = pl.BlockSpec((tm, LANES), lambda i, j: (i, 0))
    return pl.pallas_call(
        functools.partial(_mm_rope_kernel, half=half, scale0=scale0), grid=grid,
        in_specs=[x_spec, w_spec, t_spec, t_spec, t_spec], out_specs=o_spec, out_shape=o_shape,
        compiler_params=_params("parallel", "arbitrary"))(x, w, *tables)


def _mla_prep_kernel(misc_ref, gq_ref, gkv_ref, wq_ref, wk_ref, wv_ref,
                     cm_ref, s1m_ref, s2m_ref, ci_ref, s1i_ref, s2i_ref,
                     q_ref, k_ref, v_ref, ki_ref, wi_ref):
    tm = misc_ref.shape[0]
    cm, s1m, s2m = cm_ref[...], s1m_ref[...], s2m_ref[...]
    lane = lax.broadcasted_iota(jnp.int32, (tm, LANES), 1)

    cq = misc_ref[:, MISC_CQ:MISC_CQ + Q_LORA]
    cqn = cq * lax.rsqrt(jnp.mean(jnp.square(cq), axis=-1, keepdims=True) + RMS_EPS) * gq_ref[...]
    q_all = jnp.dot(cqn.astype(BF16), wq_ref[...], preferred_element_type=F32) * MLA_QSCALE
    for h in range(MLA_HEADS):
        base = h * MLA_QK_PAD
        q_ref[h, :, 0:MLA_NOPE] = q_all[:, base:base + MLA_NOPE].astype(BF16)
        pe = q_all[:, base + MLA_NOPE:base + MLA_QK_PAD]
        q_ref[h, :, MLA_NOPE:MLA_QK_PAD] = _rope_chunk(pe, cm, s1m, s2m, MLA_ROPE // 2).astype(BF16)

    ckv = misc_ref[:, MISC_CKV:MISC_CKV + KV_LORA]
    ckvn = (ckv * lax.rsqrt(jnp.mean(jnp.square(ckv), axis=-1, keepdims=True) + RMS_EPS) * gkv_ref[...]).astype(BF16)
    kn = jnp.dot(ckvn, wk_ref[...], preferred_element_type=F32)
    vn = jnp.dot(ckvn, wv_ref[...], preferred_element_type=F32)

    krc = misc_ref[:, MISC_KR:MISC_KR + LANES]
    kpe = jnp.where(lane < MLA_ROPE, _rope_chunk(krc, cm, s1m, s2m, MLA_ROPE // 2), 0.0).astype(BF16)
    for h in range(MLA_HEADS):
        k_ref[h, :, 0:MLA_NOPE] = kn[:, h * MLA_NOPE:(h + 1) * MLA_NOPE].astype(BF16)
        k_ref[h, :, MLA_NOPE:MLA_QK_PAD] = kpe
        v_ref[h] = vn[:, h * MLA_V:(h + 1) * MLA_V].astype(BF16)

    kic = misc_ref[:, MISC_KI:MISC_KI + LANES]
    ki_lo = _rope_chunk(kic, ci_ref[...], s1i_ref[...], s2i_ref[...], IDX_ROT_DIM // 2)
    ki_ref[:, 0:LANES] = ki_lo.astype(BF16)
    ki_ref[:, LANES:2 * LANES] = pltpu.roll(ki_lo, IDX_DIM, 1).astype(BF16)
    wi_ref[...] = krc[:, MLA_ROPE:MLA_ROPE + IDX_HEADS] * (IDX_HEADS ** -0.5 * IDX_DIM ** -0.5)


def _mla_prep(misc, gq, gkv, wq, wk, wv, tab_m, tab_i, tm=512):
    t = misc.shape[0]
    row = lambda w: pl.BlockSpec((tm, w), lambda i: (i, 0))
    full = lambda a: pl.BlockSpec(a.shape, lambda i: (0, 0))
    heads = lambda w: pl.BlockSpec((MLA_HEADS, tm, w), lambda i: (0, i, 0))
    return pl.pallas_call(
        _mla_prep_kernel, grid=(t // tm,),
        in_specs=[row(MISC_W), full(gq), full(gkv), full(wq), full(wk), full(wv)] + [row(LANES)] * 6,
        out_specs=[heads(MLA_QK_PAD), heads(MLA_QK_PAD), heads(MLA_V), row(2 * LANES), row(IDX_HEADS)],
        out_shape=[jax.ShapeDtypeStruct((MLA_HEADS, t, MLA_QK_PAD), BF16),
                   jax.ShapeDtypeStruct((MLA_HEADS, t, MLA_QK_PAD), BF16),
                   jax.ShapeDtypeStruct((MLA_HEADS, t, MLA_V), BF16),
                   jax.ShapeDtypeStruct((t, 2 * LANES), BF16),
                   jax.ShapeDtypeStruct((t, IDX_HEADS), F32)],
        compiler_params=_params("parallel"))(misc, gq, gkv, wq, wk, wv, *tab_m, *tab_i)


def _indexer_kernel(q_ref, w_ref, k_ref, bias_ref, sc_ref, *, tq, tk, topk, max_iter):
    i = pl.program_id(1)
    nk = bias_ref.shape[2]
    nb = (i * tq + tq - 1) // tk + 1
    kf = float(topk)
    qpos = i * tq + lax.broadcasted_iota(jnp.int32, (1, tq), 1)
    kpos0 = lax.broadcasted_iota(jnp.int32, (tk, 1), 0)
    nt = (((1,), (1,)), ((), ()))
    groups = tk // SUBLANES
    fold_rows = 4 * SUBLANES

    def fold_sum(x):
        return jnp.sum(x.reshape(tk // fold_rows, fold_rows, tq), axis=0)

    def score_blk(j, carry):
        mx, mn = carry
        start = pl.multiple_of(j * tk, tk)
        k_lo = k_ref[pl.ds(start, tk), 0:LANES]
        k_hi = k_ref[pl.ds(start, tk), LANES:2 * LANES]
        acc = jnp.zeros((tk, tq), F32)
        for p in range(IDX_HEADS // 2):
            qp = q_ref[:, p * LANES:(p + 1) * LANES]
            s0 = lax.dot_general(k_lo, qp, nt, preferred_element_type=F32)
            s1 = lax.dot_general(k_hi, qp, nt, preferred_element_type=F32)
            acc = acc + jnp.maximum(s0, 0.0) * w_ref[2 * p:2 * p + 1, :]
            acc = acc + jnp.maximum(s1, 0.0) * w_ref[2 * p + 1:2 * p + 2, :]
        causal = (j * tk + kpos0) <= qpos
        acc = acc + 0.0
        sc_ref[j] = jnp.where(causal, acc, -jnp.inf)
        mx = jnp.maximum(mx, jnp.max(jnp.where(causal, acc, -jnp.inf).reshape(groups, SUBLANES, tq), axis=0))
        mn = jnp.minimum(mn, jnp.min(jnp.where(causal, acc, jnp.inf).reshape(groups, SUBLANES, tq), axis=0))
        return mx, mn

    mx8, mn8 = lax.fori_loop(0, nb, score_blk, (jnp.full((SUBLANES, tq), -jnp.inf, F32),
                                                 jnp.full((SUBLANES, tq), jnp.inf, F32)))
    rmax = jnp.max(mx8, axis=0, keepdims=True)
    rmin = jnp.min(mn8, axis=0, keepdims=True)

    def count_ge(th):
        def body(j, c):
            for r in range(tk // fold_rows):
                c = c + jnp.where(sc_ref[j, r * fold_rows:(r + 1) * fold_rows, :] >= th, 1.0, 0.0)
            return c
        part = lax.fori_loop(0, nb, body, jnp.zeros((fold_rows, tq), F32))
        return jnp.sum(part, axis=0, keepdims=True)

    c_all = (qpos + 1).astype(F32)
    c_top = count_ge(rmax)
    top_full = c_top >= kf
    lo = jnp.where(top_full, rmax, rmin)
    c_lo = jnp.where(top_full, c_top, c_all)
    hi = jnp.where(top_full, jnp.inf, rmax)
    c_hi = jnp.where(top_full, 0.0, c_top)
    done = jnp.where(top_full | (c_lo <= kf), 1.0, 0.0)

    def cond(st):
        it, _, _, _, _, dn = st
        return jnp.logical_and(it < max_iter, jnp.min(dn) < 0.5)

    def halve(st):
        it, lo, hi, c_lo, c_hi, dn = st
        th = 0.5 * lo + 0.5 * hi
        stuck = (th <= lo) | (th >= hi)
        c = count_ge(th)
        active = jnp.logical_not(stuck) & (dn < 0.5)
        up = (c >= kf) & active
        down = (c < kf) & active
        lo = jnp.where(up, th, lo)
        c_lo = jnp.where(up, c, c_lo)
        hi = jnp.where(down, th, hi)
        c_hi = jnp.where(down, c, c_hi)
        dn = jnp.where(stuck | (c_lo <= kf), 1.0, dn)
        return it + 1, lo, hi, c_lo, c_hi, dn

    def step(st):
        return halve(halve(st))

    _, lo, hi, c_lo, c_hi, _ = lax.while_loop(cond, step, (jnp.int32(0), lo, hi, c_lo, c_hi, done))

    need = kf - c_hi
    last_key = jnp.full((1, tq), nk * tk - 1, jnp.int32)

    def band_prefix(kcut):
        def body(j, c):
            s = sc_ref[j]
            hit = (s >= lo) & (s < hi) & ((j * tk + kpos0) < kcut)
            return c + fold_sum(jnp.where(hit, 1.0, 0.0))
        part = lax.fori_loop(0, nb, body, jnp.zeros((fold_rows, tq), F32))
        return jnp.sum(part, axis=0, keepdims=True)

    def tie_search(_):
        nbits = int(nk * tk - 1).bit_length()
        kcut = jnp.zeros((1, tq), jnp.int32)
        for b in range(nbits - 1, -1, -1):
            trial = kcut | (1 << b)
            kcut = jnp.where(band_prefix(trial) < need, trial, kcut)
        return kcut

    has_excess = jnp.max(c_lo) > kf
    kcut = lax.cond(has_excess, tie_search, lambda _: last_key, 0)
    kcut = jnp.where(c_lo > kf, kcut, last_key)

    def emit(j, carry):
        s = sc_ref[j]
        sel = (s >= lo) & ((s >= hi) | ((j * tk + kpos0) <= kcut))
        bias_ref[0, 0, j] = jnp.where(sel, 0.0, MASK_NEG).T.astype(bias_ref.dtype)
        return carry

    lax.fori_loop(0, nb, emit, 0)

    def fill(j, carry):
        bias_ref[0, 0, j] = jnp.full((tq, tk), MASK_NEG, bias_ref.dtype)
        return carry

    lax.fori_loop(nb, nk, fill, 0)


def _indexer(qi, wi_t, ki2, batch, seq, topk, tq=ATTN_TQ, tk=ATTN_TK):
    nq, nk = seq // tq, seq // tk
    kern = functools.partial(_indexer_kernel, tq=tq, tk=tk, topk=topk, max_iter=128)
    return pl.pallas_call(
        kern, grid=(batch, nq),
        in_specs=[pl.BlockSpec((tq, IDX_HEADS * IDX_DIM), lambda b, i: (b * nq + i, 0)),
                  pl.BlockSpec((IDX_HEADS, tq), lambda b, i: (0, b * nq + i)),
                  pl.BlockSpec((seq, 2 * LANES), lambda b, i: (b, 0))],
        out_specs=pl.BlockSpec((1, 1, nk, tq, tk), lambda b, i: (b, i, 0, 0, 0)),
        out_shape=jax.ShapeDtypeStruct((batch, nq, nk, tq, tk), BF16),
        scratch_shapes=[pltpu.VMEM((nk, tk, tq), F32)],
        compiler_params=_params("parallel", "arbitrary"))(qi, wi_t, ki2)


def _attn_kernel(it_ref, jt_ref, fl_ref, q_ref, k_ref, v_ref, *rest, heads, use_bias):
    if use_bias:
        bias_ref, o_ref, m_ref, l_ref, al_ref, acc_ref, s_ref, p_ref, bf_ref = rest
    else:
        o_ref, m_ref, l_ref, al_ref, acc_ref, s_ref, p_ref, bf_ref = rest
    step = pl.program_id(1)
    i = it_ref[step]
    j = jt_ref[step]
    flags = fl_ref[step]
    tq = q_ref.shape[1]
    tk = k_ref.shape[1]
    nch = tk // LANES
    nt = (((1,), (1,)), ((), ()))

    @pl.when(j == 0)
    def _():
        m_ref[...] = jnp.full(m_ref.shape, -jnp.inf, F32)
        l_ref[...] = jnp.zeros(l_ref.shape, F32)
        acc_ref[...] = jnp.zeros(acc_ref.shape, F32)

    def logits(h, slot, biased):
        s = lax.dot_general(q_ref[h], k_ref[h], nt, preferred_element_type=F32)
        if biased:
            s = s + bf_ref[...]
        s_ref[slot] = s

    def softmax(h, slot):
        mc = s_ref[slot, :, 0:LANES]
        for c in range(1, nch):
            mc = jnp.maximum(mc, s_ref[slot, :, c * LANES:(c + 1) * LANES])
        m_prev = m_ref[h]
        m_new = jnp.maximum(m_prev, jnp.max(mc, axis=1, keepdims=True))
        alpha = jnp.exp2(m_prev - m_new)
        m_ref[h] = m_new
        al_ref[h] = alpha
        lsum = None
        for c in range(nch):
            cols = slice(c * LANES, (c + 1) * LANES)
            pc = jnp.exp2(s_ref[slot, :, cols] - m_new)
            lsum = pc if lsum is None else lsum + pc
            p_ref[slot, :, cols] = pc.astype(BF16)
        l_ref[h] = alpha * l_ref[h] + jnp.sum(lsum, axis=1, keepdims=True)

    def weighted_values(h, slot):
        pv = jnp.dot(p_ref[slot], v_ref[h], preferred_element_type=F32)
        acc_ref[h] = al_ref[h] * acc_ref[h] + pv

    def body(biased):
        logits(0, 0, biased)
        logits(1, 1, biased)
        softmax(0, 0)

        for h in range(1, heads - 1):
            cur = h % 2
            logits(h + 1, 1 - cur, biased)
            softmax(h, cur)
            weighted_values(h - 1, 1 - cur)
        last = (heads - 1) % 2
        softmax(heads - 1, last)
        weighted_values(heads - 2, 1 - last)
        weighted_values(heads - 1, last)

    if use_bias:
        bf_ref[...] = bias_ref[0, 0, 0].astype(F32)
        body(True)
    else:
        diag = (flags & 2) != 0

        @pl.when(diag)
        def _():
            row = i * tq + lax.broadcasted_iota(jnp.int32, (tq, 1), 0)
            col = j * tk + lax.broadcasted_iota(jnp.int32, (1, tk), 1)
            bf_ref[...] = jnp.where(col <= row, 0.0, MASK_NEG)
            body(True)

        pl.when(jnp.logical_not(diag))(lambda: body(False))

    @pl.when((flags & 1) != 0)
    def _():
        for h in range(heads):
            o_ref[:, h * LANES:(h + 1) * LANES] = (acc_ref[h] * (1.0 / l_ref[h])).astype(o_ref.dtype)


def _attention(q, k, v, bias, batch, seq, q_head0=0, k_head0=0, heads=8, tq=ATTN_TQ, tk=ATTN_TK):
    dqk, dv = q.shape[2], v.shape[2]
    assert dv == LANES and q_head0 % heads == 0 and k_head0 % heads == 0
    qb, kb = q_head0 // heads, k_head0 // heads
    nq, nk = seq // tq, seq // tk
    pairs = [(i, j) for i in range(nq) for j in range((i * tq + tq - 1) // tk + 1)]
    it = jnp.asarray([p[0] for p in pairs], jnp.int32)
    jt = jnp.asarray([p[1] for p in pairs], jnp.int32)
    fl = jnp.asarray([(1 if j == (i * tq + tq - 1) // tk else 0) + (2 if (j + 1) * tk - 1 > i * tq else 0)
                      for i, j in pairs], jnp.int32)
    use_bias = bias is not None
    in_specs = [pl.BlockSpec((heads, tq, dqk), lambda b, s, it, jt, fl: (qb, b * nq + it[s], 0)),
                pl.BlockSpec((heads, tk, dqk), lambda b, s, it, jt, fl: (kb, b * nk + jt[s], 0)),
                pl.BlockSpec((heads, tk, dv), lambda b, s, it, jt, fl: (0, b * nk + jt[s], 0))]
    args = [q, k, v]
    if use_bias:
        in_specs.append(pl.BlockSpec((1, 1, 1, tq, tk), lambda b, s, it, jt, fl: (b, it[s], jt[s], 0, 0)))
        args.append(bias)
    stat = pltpu.VMEM((heads, tq, LANES), F32)
    scratch = [stat, stat, stat, stat, pltpu.VMEM((2, tq, tk), F32), pltpu.VMEM((2, tq, tk), BF16),
               pltpu.VMEM((tq, tk), F32)]
    return pl.pallas_call(
        functools.partial(_attn_kernel, heads=heads, use_bias=use_bias),
        grid_spec=pltpu.PrefetchScalarGridSpec(
            num_scalar_prefetch=3, grid=(batch, len(pairs)), in_specs=in_specs,
            out_specs=pl.BlockSpec((tq, heads * dv), lambda b, s, it, jt, fl: (b * nq + it[s], 0)),
            scratch_shapes=scratch),
        out_shape=jax.ShapeDtypeStruct((batch * seq, heads * dv), BF16),
        compiler_params=_params("parallel", "arbitrary"))(it, jt, fl, *args)


def _residual_ln(x, upd, g, b, alpha):
    y = alpha * x + upd
    mu = jnp.mean(y, axis=-1, keepdims=True)
    d = y - mu
    var = jnp.mean(jnp.square(d), axis=-1, keepdims=True)
    return d * lax.rsqrt(var + LN_EPS) * g + b


def _outproj_ln_kernel(a_ref, b_ref, wa_ref, wb_ref, x_ref, g_ref, beta_ref, o_ref, ob_ref, *, alpha):
    mix = jnp.dot(a_ref[...], wa_ref[...], preferred_element_type=F32)
    mix = mix + jnp.dot(b_ref[...], wb_ref[...], preferred_element_type=F32)
    y = _residual_ln(x_ref[...], mix, g_ref[...], beta_ref[...], alpha)
    o_ref[...] = y
    ob_ref[...] = y.astype(BF16)


def _outproj_ln(a, b, w_o, layer, x, g, beta, alpha, tm=512):
    t, d = x.shape
    wa_rows = a.shape[1]
    row = lambda w: pl.BlockSpec((tm, w), lambda i: (i, 0))
    full = lambda arr: pl.BlockSpec(arr.shape, lambda i: (0, 0))
    half = lambda r: pl.BlockSpec((None, wa_rows, d), lambda i: (layer, r, 0))
    return pl.pallas_call(
        functools.partial(_outproj_ln_kernel, alpha=alpha), grid=(t // tm,),
        in_specs=[row(wa_rows), row(b.shape[1]), half(0), half(1), row(d), full(g), full(beta)],
        out_specs=[row(d), row(d)],
        out_shape=[jax.ShapeDtypeStruct((t, d), F32), jax.ShapeDtypeStruct((t, d), BF16)],
        compiler_params=_params("parallel"))(a, b, w_o, w_o, x, g, beta)


def _ffn_up_kernel(x_ref, wg_ref, wu_ref, h_ref):
    xb = x_ref[...]
    g = jnp.dot(xb, wg_ref[...], preferred_element_type=F32)
    u = jnp.dot(xb, wu_ref[...], preferred_element_type=F32)
    h_ref[...] = (g * (1.0 / (1.0 + jnp.exp(-g))) * u).astype(h_ref.dtype)


def _ffn_up(x, wg, wu, layer, tm=2048, tn=512):
    t, d = x.shape
    f = wg.shape[2]
    return pl.pallas_call(
        _ffn_up_kernel, grid=(t // tm, f // tn),
        in_specs=[pl.BlockSpec((tm, d), lambda i, j: (i, 0)),
                  pl.BlockSpec((None, d, tn), lambda i, j: (layer, 0, j)),
                  pl.BlockSpec((None, d, tn), lambda i, j: (layer, 0, j))],
        out_specs=pl.BlockSpec((tm, tn), lambda i, j: (i, j)),
        out_shape=jax.ShapeDtypeStruct((t, f), BF16),
        compiler_params=_params("parallel", "arbitrary"))(x, wg, wu)


def _ffn_down_ln_kernel(h_ref, w_ref, x_ref, g_ref, beta_ref, o_ref, *rest, alpha):
    ob_ref, acc_ref = rest if len(rest) == 2 else (None, rest[0])
    k = pl.program_id(1)

    @pl.when(k == 0)
    def _():
        acc_ref[...] = jnp.zeros(acc_ref.shape, F32)

    acc_ref[...] += jnp.dot(h_ref[...], w_ref[...], preferred_element_type=F32)

    @pl.when(k == pl.num_programs(1) - 1)
    def _():
        y = _residual_ln(x_ref[...], acc_ref[...], g_ref[...], beta_ref[...], alpha)
        o_ref[...] = y
        if ob_ref is not None:
            ob_ref[...] = y.astype(BF16)


def _ffn_down_ln(h, w, layer, x, g, beta, alpha, with_copy, tm=512, tk=1408):
    t, d = x.shape
    f = h.shape[1]
    rows = lambda: pl.BlockSpec((tm, d), lambda i, k: (i, 0))
    return pl.pallas_call(
        functools.partial(_ffn_down_ln_kernel, alpha=alpha), grid=(t // tm, f // tk),
        in_specs=[pl.BlockSpec((tm, tk), lambda i, k: (i, k)),
                  pl.BlockSpec((None, tk, d), lambda i, k: (layer, k, 0)),
                  rows(),
                  pl.BlockSpec((1, d), lambda i, k: (0, 0)),
                  pl.BlockSpec((1, d), lambda i, k: (0, 0))],
        out_specs=[rows(), rows()][:2 if with_copy else 1],
        out_shape=[jax.ShapeDtypeStruct((t, d), F32), jax.ShapeDtypeStruct((t, d), BF16)][:2 if with_copy else 1],
        scratch_shapes=[pltpu.VMEM((tm, d), F32)],
        compiler_params=_params("parallel", "arbitrary"))(h, w, x, g, beta)


def _pack_small_weights(w_in, w_uq, w_ukv):
    sl = lambda k: w_in[:, _OFF[k] - _OFF[4]:_OFF[k + 1] - _OFF[4]]
    wi, ki, cq, ckv, kr = (sl(k) for k in range(4, 9))
    d = w_in.shape[0]
    zeros = lambda n: jnp.zeros((d, n), w_in.dtype)
    w_misc = jnp.concatenate([cq, ckv, kr, wi, zeros(LANES - MLA_ROPE - IDX_HEADS), ki, zeros(LANES - IDX_DIM)], axis=1)
    uq = w_uq.reshape(Q_LORA, MLA_HEADS, MLA_NOPE + MLA_ROPE)
    uq = jnp.pad(uq, ((0, 0), (0, 0), (0, MLA_QK_PAD - MLA_NOPE - MLA_ROPE))).reshape(Q_LORA, MLA_HEADS * MLA_QK_PAD)
    ukv = w_ukv.reshape(KV_LORA, MLA_HEADS, MLA_NOPE + MLA_V)
    uk = ukv[:, :, :MLA_NOPE].reshape(KV_LORA, MLA_HEADS * MLA_NOPE)
    uv = ukv[:, :, MLA_NOPE:].reshape(KV_LORA, MLA_HEADS * MLA_V)
    return w_misc, uq, uk, uv


def kernel(x, positions, w_in, g_cq, g_ckv, w_uq, w_ukv, w_o, ln1_g, ln1_b, w_gate, w_up, w_down, ln2_g, ln2_b):
    batch, seq, d = x.shape
    depth = w_in.shape[0]
    alpha = (2 * depth) ** 0.25
    topk = min(TOPK_MAX, seq // 4)
    tab_a = _rope_tables(positions, A_ROT_DIM, ROPE_THETA, A_HEAD_DIM)
    tab_i = _rope_tables(positions, IDX_ROT_DIM, ROPE_THETA, IDX_DIM)
    tab_m = _rope_tables(positions, MLA_ROPE, MLA_ROPE_THETA, LANES)
    xt = x.reshape(batch * seq, d)
    xb = xt.astype(BF16)
    w_in_b, w_tail_b = w_in[:, :, :_OFF[4]].astype(BF16), w_in[:, :, _OFF[4]:].astype(BF16)
    w_uq_b, w_ukv_b, w_o_b = (w.astype(BF16) for w in (w_uq, w_ukv, w_o))
    w_gate_b, w_up_b, w_down_b = (w.astype(BF16) for w in (w_gate, w_up, w_down))
    tile = A_WIDTH
    for l in range(depth):
        w_misc, uq, uk, uv = _pack_small_weights(w_tail_b[l], w_uq_b[l], w_ukv_b[l])
        qk = _project(xb, w_in_b, BF16, tab_a, A_ROT_DIM // 2, scale0=A_QSCALE, head_major=True,
                      layer=l, col0=0, n=2 * tile, tn=tile)
        va = _project(xb, w_in_b, BF16, head_major=True, layer=l, col0=2, n=tile, tn=tile)
        qi = _project(xb, w_in_b, BF16, tab_i, IDX_ROT_DIM // 2, layer=l, col0=3, n=tile, tn=tile)
        misc = _project(xb, w_misc, F32)
        q_mla, k_mla, v_mla, ki2, wi = _mla_prep(
            misc, g_cq[l].reshape(1, -1), g_ckv[l].reshape(1, -1), uq, uk, uv, tab_m, tab_i)
        bias = _indexer(qi, wi.T, ki2, batch, seq, topk)
        out_a = _attention(qk, qk, va, bias, batch, seq, q_head0=0, k_head0=A_HEADS, heads=A_HEADS)
        out_b = _attention(q_mla, k_mla, v_mla, None, batch, seq, heads=MLA_HEADS)
        xt, xb = _outproj_ln(out_a, out_b, w_o_b, l, xt,
                             ln1_g[l].reshape(1, -1), ln1_b[l].reshape(1, -1), alpha)
        h = _ffn_up(xb, w_gate_b, w_up_b, l)
        outs = _ffn_down_ln(h, w_down_b, l, xt, ln2_g[l].reshape(1, -1), ln2_b[l].reshape(1, -1), alpha,
                            with_copy=l + 1 < depth)
        xt, xb = outs if l + 1 < depth else (outs[0], None)
    return xt.reshape(batch, seq, d)
```

```python
import functools
import math

import numpy as np
import jax
import jax.numpy as jnp
from jax import lax
from jax.experimental import pallas as pl
from jax.experimental.pallas import tpu as pltpu

F32 = jnp.float32
BF16 = jnp.bfloat16

D_MODEL = 2048
A_HEADS = 8
A_HEAD_DIM = 128
A_ROT_DIM = A_HEAD_DIM // 4
IDX_HEADS = 16
IDX_DIM = 64
IDX_ROT_DIM = IDX_DIM // 4
TOPK_MAX = 256
MLA_HEADS = 8
MLA_NOPE = 128
MLA_ROPE = 64
MLA_V = 128
MLA_QK_PAD = 256
Q_LORA = 512
KV_LORA = 256
FFN_DIM = 5632
ROPE_THETA = 500000.0
MLA_ROPE_THETA = 10000.0
LN_EPS = 1e-5
RMS_EPS = 1e-6
A_WIDTH = A_HEADS * A_HEAD_DIM
LANES = 128
SUBLANES = 8
MASK_NEG = -1e30
VMEM_LIMIT = 56 * 1024 * 1024
LOG2E = math.log2(math.e)
A_QSCALE = A_HEAD_DIM ** -0.5 * LOG2E
MLA_QSCALE = (MLA_NOPE + MLA_ROPE) ** -0.5 * LOG2E
LN_PANEL = 128
ATTN_TQ = 512
ATTN_TK = 512

_OFF = np.cumsum([0, A_WIDTH, A_WIDTH, A_WIDTH, IDX_HEADS * IDX_DIM, IDX_HEADS, IDX_DIM,
                  Q_LORA, KV_LORA, MLA_ROPE]).tolist()
MISC_W = 1024
MISC_CQ = 0
MISC_CKV = Q_LORA
MISC_KR = Q_LORA + KV_LORA
MISC_KI = MISC_KR + LANES


def _params(*sem):
    return pltpu.CompilerParams(dimension_semantics=sem, vmem_limit_bytes=VMEM_LIMIT)


def _rope_tables(positions, rot_dim, theta, period):
    half = rot_dim // 2
    inv_freq = theta ** (-2.0 * jnp.arange(half, dtype=F32) / rot_dim)
    ang = inv_freq[:, None] * positions.astype(F32).reshape(1, -1)
    cos, sin = lax.optimization_barrier((jnp.cos(ang), jnp.sin(ang)))
    t = cos.shape[1]
    rest = period - rot_dim
    z_h = jnp.zeros((half, t), F32)
    z_r = jnp.zeros((rest, t), F32)
    c = jnp.concatenate([cos, cos, jnp.ones((rest, t), F32)], axis=0)
    s1 = jnp.concatenate([-sin, z_h, z_r], axis=0)
    s2 = jnp.concatenate([z_h, sin, z_r], axis=0)
    rep = LANES // period
    return tuple(jnp.tile(a, (rep, 1)).T for a in (c, s1, s2))


def _rope_chunk(a, c, s1, s2, half):
    return a * c + pltpu.roll(a, LANES - half, 1) * s1 + pltpu.roll(a, half, 1) * s2


def _store_cols(o_ref, k, val):
    if len(o_ref.shape) == 3:
        o_ref[k] = val.astype(o_ref.dtype)
    else:
        o_ref[:, k * LANES:(k + 1) * LANES] = val.astype(o_ref.dtype)


def _mm_kernel(x_ref, w_ref, o_ref):
    acc = jnp.dot(x_ref[...].astype(BF16), w_ref[...].astype(BF16), preferred_element_type=F32)
    if len(o_ref.shape) == 3:
        for k in range(acc.shape[1] // LANES):
            _store_cols(o_ref, k, acc[:, k * LANES:(k + 1) * LANES])
    else:
        o_ref[...] = acc.astype(o_ref.dtype)


def _mm_rope_kernel(x_ref, w_ref, c_ref, s1_ref, s2_ref, o_ref, *, half, scale0):
    acc = jnp.dot(x_ref[...].astype(BF16), w_ref[...].astype(BF16), preferred_element_type=F32)
    if scale0 is not None:
        acc = acc * jnp.where(pl.program_id(1) == 0, scale0, 1.0)
    c, s1, s2 = c_ref[...], s1_ref[...], s2_ref[...]
    for k in range(acc.shape[1] // LANES):
        _store_cols(o_ref, k, _rope_chunk(acc[:, k * LANES:(k + 1) * LANES], c, s1, s2, half))


def _project(x, w, out_dtype, tables=None, half=0, scale0=None, head_major=False, layer=0, col0=0, n=None,
             tm=1024, tn=1024):
    t, kdim = x.shape
    n = w.shape[-1] if n is None else n
    grid = (t // tm, n // tn)
    x_spec = pl.BlockSpec((tm, kdim), lambda i, j: (i, 0))
    if w.ndim == 3:
        w_spec = pl.BlockSpec((None, kdim, tn), lambda i, j: (layer, 0, col0 + j))
    else:
        w_spec = pl.BlockSpec((kdim, tn), lambda i, j: (0, j))
    if head_major:
        o_spec = pl.BlockSpec((tn // LANES, tm, LANES), lambda i, j: (j, i, 0))
        o_shape = jax.ShapeDtypeStruct((n // LANES, t, LANES), out_dtype)
    else:
        o_spec = pl.BlockSpec((tm, tn), lambda i, j: (i, j))
        o_shape = jax.ShapeDtypeStruct((t, n), out_dtype)
    if tables is None:
        return pl.pallas_call(
            _mm_kernel, grid=grid, in_specs=[x_spec, w_spec], out_specs=o_spec, out_shape=o_shape,
            compiler_params=_params("parallel", "arbitrary"))(x, w)
    t_spec = pl.BlockSpec((tm, LANES), lambda i, j: (i, 0))
    return pl.pallas_call(
        functools.partial(_mm_rope_kernel, half=half, scale0=scale0), grid=grid,
        in_specs=[x_spec, w_spec, t_spec, t_spec, t_spec], out_specs=o_spec, out_shape=o_shape,
        compiler_params=_params("parallel", "arbitrary"))(x, w, *tables)


def _mla_prep_kernel(misc_ref, gq_ref, gkv_ref, wq_ref, wk_ref, wv_ref,
                     cm_ref, s1m_ref, s2m_ref, ci_ref, s1i_ref, s2i_ref,
                     q_ref, k_ref, v_ref, ki_ref, wi_ref):
    tm = misc_ref.shape[0]
    cm, s1m, s2m = cm_ref[...], s1m_ref[...], s2m_ref[...]
    lane = lax.broadcasted_iota(jnp.int32, (tm, LANES), 1)

    cq = misc_ref[:, MISC_CQ:MISC_CQ + Q_LORA]
    cqn = cq * lax.rsqrt(jnp.mean(jnp.square(cq), axis=-1, keepdims=True) + RMS_EPS) * gq_ref[...]
    q_all = jnp.dot(cqn.astype(BF16), wq_ref[...], preferred_element_type=F32) * MLA_QSCALE
    for h in range(MLA_HEADS):
        base = h * MLA_QK_PAD
        q_ref[h, :, 0:MLA_NOPE] = q_all[:, base:base + MLA_NOPE].astype(BF16)
        pe = q_all[:, base + MLA_NOPE:base + MLA_QK_PAD]
        q_ref[h, :, MLA_NOPE:MLA_QK_PAD] = _rope_chunk(pe, cm, s1m, s2m, MLA_ROPE // 2).astype(BF16)

    ckv = misc_ref[:, MISC_CKV:MISC_CKV + KV_LORA]
    ckvn = (ckv * lax.rsqrt(jnp.mean(jnp.square(ckv), axis=-1, keepdims=True) + RMS_EPS) * gkv_ref[...]).astype(BF16)
    kn = jnp.dot(ckvn, wk_ref[...], preferred_element_type=F32)
    vn = jnp.dot(ckvn, wv_ref[...], preferred_element_type=F32)

    krc = misc_ref[:, MISC_KR:MISC_KR + LANES]
    kpe = jnp.where(lane < MLA_ROPE, _rope_chunk(krc, cm, s1m, s2m, MLA_ROPE // 2), 0.0).astype(BF16)
    for h in range(MLA_HEADS):
        k_ref[h, :, 0:MLA_NOPE] = kn[:, h * MLA_NOPE:(h + 1) * MLA_NOPE].astype(BF16)
        k_ref[h, :, MLA_NOPE:MLA_QK_PAD] = kpe
        v_ref[h] = vn[:, h * MLA_V:(h + 1) * MLA_V].astype(BF16)

    kic = misc_ref[:, MISC_KI:MISC_KI + LANES]
    ki_lo = _rope_chunk(kic, ci_ref[...], s1i_ref[...], s2i_ref[...], IDX_ROT_DIM // 2)
    ki_ref[:, 0:LANES] = ki_lo.astype(BF16)
    ki_ref[:, LANES:2 * LANES] = pltpu.roll(ki_lo, IDX_DIM, 1).astype(BF16)
    wi_ref[...] = krc[:, MLA_ROPE:MLA_ROPE + IDX_HEADS] * (IDX_HEADS ** -0.5 * IDX_DIM ** -0.5)


def _mla_prep(misc, gq, gkv, wq, wk, wv, tab_m, tab_i, tm=512):
    t = misc.shape[0]
    row = lambda w: pl.BlockSpec((tm, w), lambda i: (i, 0))
    full = lambda a: pl.BlockSpec(a.shape, lambda i: (0, 0))
    heads = lambda w: pl.BlockSpec((MLA_HEADS, tm, w), lambda i: (0, i, 0))
    return pl.pallas_call(
        _mla_prep_kernel, grid=(t // tm,),
        in_specs=[row(MISC_W), full(gq), full(gkv), full(wq), full(wk), full(wv)] + [row(LANES)] * 6,
        out_specs=[heads(MLA_QK_PAD), heads(MLA_QK_PAD), heads(MLA_V), row(2 * LANES), row(IDX_HEADS)],
        out_shape=[jax.ShapeDtypeStruct((MLA_HEADS, t, MLA_QK_PAD), BF16),
                   jax.ShapeDtypeStruct((MLA_HEADS, t, MLA_QK_PAD), BF16),
                   jax.ShapeDtypeStruct((MLA_HEADS, t, MLA_V), BF16),
                   jax.ShapeDtypeStruct((t, 2 * LANES), BF16),
                   jax.ShapeDtypeStruct((t, IDX_HEADS), F32)],
        compiler_params=_params("parallel"))(misc, gq, gkv, wq, wk, wv, *tab_m, *tab_i)


def _indexer_kernel(q_ref, w_ref, k_ref, bias_ref, sc_ref, *, tq, tk, topk, max_iter):
    i = pl.program_id(1)
    nk = bias_ref.shape[2]
    nb = (i * tq + tq - 1) // tk + 1
    kf = float(topk)
    qpos = i * tq + lax.broadcasted_iota(jnp.int32, (1, tq), 1)
    kpos0 = lax.broadcasted_iota(jnp.int32, (tk, 1), 0)
    nt = (((1,), (1,)), ((), ()))
    groups = tk // SUBLANES
    fold_rows = 4 * SUBLANES

    def fold_sum(x):
        return jnp.sum(x.reshape(tk // fold_rows, fold_rows, tq), axis=0)

    def score_blk(j, carry):
        mx, mn = carry
        start = pl.multiple_of(j * tk, tk)
        k_lo = k_ref[pl.ds(start, tk), 0:LANES]
        k_hi = k_ref[pl.ds(start, tk), LANES:2 * LANES]
        acc = jnp.zeros((tk, tq), F32)
        for p in range(IDX_HEADS // 2):
            qp = q_ref[:, p * LANES:(p + 1) * LANES]
            s0 = lax.dot_general(k_lo, qp, nt, preferred_element_type=F32)
            s1 = lax.dot_general(k_hi, qp, nt, preferred_element_type=F32)
            acc = acc + jnp.maximum(s0, 0.0) * w_ref[2 * p:2 * p + 1, :]
            acc = acc + jnp.maximum(s1, 0.0) * w_ref[2 * p + 1:2 * p + 2, :]
        causal = (j * tk + kpos0) <= qpos
        acc = acc + 0.0
        sc_ref[j] = jnp.where(causal, acc, -jnp.inf)
        mx = jnp.maximum(mx, jnp.max(jnp.where(causal, acc, -jnp.inf).reshape(groups, SUBLANES, tq), axis=0))
        mn = jnp.minimum(mn, jnp.min(jnp.where(causal, acc, jnp.inf).reshape(groups, SUBLANES, tq), axis=0))
        return mx, mn

    mx8, mn8 = lax.fori_loop(0, nb, score_blk, (jnp.full((SUBLANES, tq), -jnp.inf, F32),
                                                 jnp.full((SUBLANES, tq), jnp.inf, F32)))
    rmax = jnp.max(mx8, axis=0, keepdims=True)
    rmin = jnp.min(mn8, axis=0, keepdims=True)

    def count_ge(th):
        def body(j, c):
            for r in range(tk // fold_rows):
                c = c + jnp.where(sc_ref[j, r * fold_rows:(r + 1) * fold_rows, :] >= th, 1.0, 0.0)
            return c
        part = lax.fori_loop(0, nb, body, jnp.zeros((fold_rows, tq), F32))
        return jnp.sum(part, axis=0, keepdims=True)

    c_all = (qpos + 1).astype(F32)
    c_top = count_ge(rmax)
    top_full = c_top >= kf
    lo = jnp.where(top_full, rmax, rmin)
    c_lo = jnp.where(top_full, c_top, c_all)
    hi = jnp.where(top_full, jnp.inf, rmax)
    c_hi = jnp.where(top_full, 0.0, c_top)
    done = jnp.where(top_full | (c_lo <= kf), 1.0, 0.0)

    def cond(st):
        it, _, _, _, _, dn = st
        return jnp.logical_and(it < max_iter, jnp.min(dn) < 0.5)

    def halve(st):
        it, lo, hi, c_lo, c_hi, dn = st
        th = 0.5 * lo + 0.5 * hi
        stuck = (th <= lo) | (th >= hi)
        c = count_ge(th)
        active = jnp.logical_not(stuck) & (dn < 0.5)
        up = (c >= kf) & active
        down = (c < kf) & active
        lo = jnp.where(up, th, lo)
        c_lo = jnp.where(up, c, c_lo)
        hi = jnp.where(down, th, hi)
        c_hi = jnp.where(down, c, c_hi)
        dn = jnp.where(stuck | (c_lo <= kf), 1.0, dn)
        return it + 1, lo, hi, c_lo, c_hi, dn

    def step(st):
        return halve(halve(st))

    _, lo, hi, c_lo, c_hi, _ = lax.while_loop(cond, step, (jnp.int32(0), lo, hi, c_lo, c_hi, done))

    need = kf - c_hi
    last_key = jnp.full((1, tq), nk * tk - 1, jnp.int32)

    def band_prefix(kcut):
        def body(j, c):
            s = sc_ref[j]
            hit = (s >= lo) & (s < hi) & ((j * tk + kpos0) < kcut)
            return c + fold_sum(jnp.where(hit, 1.0, 0.0))
        part = lax.fori_loop(0, nb, body, jnp.zeros((fold_rows, tq), F32))
        return jnp.sum(part, axis=0, keepdims=True)

    def tie_search(_):
        nbits = int(nk * tk - 1).bit_length()
        kcut = jnp.zeros((1, tq), jnp.int32)
        for b in range(nbits - 1, -1, -1):
            trial = kcut | (1 << b)
            kcut = jnp.where(band_prefix(trial) < need, trial, kcut)
        return kcut

    has_excess = jnp.max(c_lo) > kf
    kcut = lax.cond(has_excess, tie_search, lambda _: last_key, 0)
    kcut = jnp.where(c_lo > kf, kcut, last_key)

    def emit(j, carry):
        s = sc_ref[j]
        sel = (s >= lo) & ((s >= hi) | ((j * tk + kpos0) <= kcut))
        bias_ref[0, 0, j] = jnp.where(sel, 0.0, MASK_NEG).T.astype(bias_ref.dtype)
        return carry

    lax.fori_loop(0, nb, emit, 0)

    def fill(j, carry):
        bias_ref[0, 0, j] = jnp.full((tq, tk), MASK_NEG, bias_ref.dtype)
        return carry

    lax.fori_loop(nb, nk, fill, 0)


def _indexer(qi, wi_t, ki2, batch, seq, topk, tq=ATTN_TQ, tk=ATTN_TK):
    nq, nk = seq // tq, seq // tk
    kern = functools.partial(_indexer_kernel, tq=tq, tk=tk, topk=topk, max_iter=128)
    return pl.pallas_call(
        kern, grid=(batch, nq),
        in_specs=[pl.BlockSpec((tq, IDX_HEADS * IDX_DIM), lambda b, i: (b * nq + i, 0)),
                  pl.BlockSpec((IDX_HEADS, tq), lambda b, i: (0, b * nq + i)),
                  pl.BlockSpec((seq, 2 * LANES), lambda b, i: (b, 0))],
        out_specs=pl.BlockSpec((1, 1, nk, tq, tk), lambda b, i: (b, i, 0, 0, 0)),
        out_shape=jax.ShapeDtypeStruct((batch, nq, nk, tq, tk), BF16),
        scratch_shapes=[pltpu.VMEM((nk, tk, tq), F32)],
        compiler_params=_params("parallel", "arbitrary"))(qi, wi_t, ki2)


def _attn_kernel(it_ref, jt_ref, fl_ref, q_ref, k_ref, v_ref, *rest, heads, use_bias):
    if use_bias:
        bias_ref, o_ref, m_ref, l_ref, al_ref, acc_ref, s_ref, p_ref, bf_ref = rest
    else:
        o_ref, m_ref, l_ref, al_ref, acc_ref, s_ref, p_ref, bf_ref = rest
    step = pl.program_id(1)
    i = it_ref[step]
    j = jt_ref[step]
    flags = fl_ref[step]
    tq = q_ref.shape[1]
    tk = k_ref.shape[1]
    nch = tk // LANES
    nt = (((1,), (1,)), ((), ()))

    @pl.when(j == 0)
    def _():
        m_ref[...] = jnp.full(m_ref.shape, -jnp.inf, F32)
        l_ref[...] = jnp.zeros(l_ref.shape, F32)
        acc_ref[...] = jnp.zeros(acc_ref.shape, F32)

    def logits(h, slot, biased):
        s = lax.dot_general(q_ref[h], k_ref[h], nt, preferred_element_type=F32)
        if biased:
            s = s + bf_ref[...]
        s_ref[slot] = s

    def softmax(h, slot):
        mc = s_ref[slot, :, 0:LANES]
        for c in range(1, nch):
            mc = jnp.maximum(mc, s_ref[slot, :, c * LANES:(c + 1) * LANES])
        m_prev = m_ref[h]
        m_new = jnp.maximum(m_prev, jnp.max(mc, axis=1, keepdims=True))
        alpha = jnp.exp2(m_prev - m_new)
        m_ref[h] = m_new
        al_ref[h] = alpha
        lsum = None
        for c in range(nch):
            cols = slice(c * LANES, (c + 1) * LANES)
            pc = jnp.exp2(s_ref[slot, :, cols] - m_new)
            lsum = pc if lsum is None else lsum + pc
            p_ref[slot, :, cols] = pc.astype(BF16)
        l_ref[h] = alpha * l_ref[h] + jnp.sum(lsum, axis=1, keepdims=True)

    def weighted_values(h, slot):
        pv = jnp.dot(p_ref[slot], v_ref[h], preferred_element_type=F32)
        acc_ref[h] = al_ref[h] * acc_ref[h] + pv

    def body(biased):
        logits(0, 0, biased)
        logits(1, 1, biased)
        softmax(0, 0)

        for h in range(1, heads - 1):
            cur = h % 2
            logits(h + 1, 1 - cur, biased)
            softmax(h, cur)
            weighted_values(h - 1, 1 - cur)
        last = (heads - 1) % 2
        softmax(heads - 1, last)
        weighted_values(heads - 2, 1 - last)
        weighted_values(heads - 1, last)

    if use_bias:
        bf_ref[...] = bias_ref[0, 0, 0].astype(F32)
        body(True)
    else:
        diag = (flags & 2) != 0

        @pl.when(diag)
        def _():
            row = i * tq + lax.broadcasted_iota(jnp.int32, (tq, 1), 0)
            col = j * tk + lax.broadcasted_iota(jnp.int32, (1, tk), 1)
            bf_ref[...] = jnp.where(col <= row, 0.0, MASK_NEG)
            body(True)

        pl.when(jnp.logical_not(diag))(lambda: body(False))

    @pl.when((flags & 1) != 0)
    def _():
        for h in range(heads):
            o_ref[:, h * LANES:(h + 1) * LANES] = (acc_ref[h] * (1.0 / l_ref[h])).astype(o_ref.dtype)


def _attention(q, k, v, bias, batch, seq, q_head0=0, k_head0=0, heads=8, tq=ATTN_TQ, tk=ATTN_TK):
    dqk, dv = q.shape[2], v.shape[2]
    assert dv == LANES and q_head0 % heads == 0 and k_head0 % heads == 0
    qb, kb = q_head0 // heads, k_head0 // heads
    nq, nk = seq // tq, seq // tk
    pairs = [(i, j) for i in range(nq) for j in range((i * tq + tq - 1) // tk + 1)]
    it = jnp.asarray([p[0] for p in pairs], jnp.int32)
    jt = jnp.asarray([p[1] for p in pairs], jnp.int32)
    fl = jnp.asarray([(1 if j == (i * tq + tq - 1) // tk else 0) + (2 if (j + 1) * tk - 1 > i * tq else 0)
                      for i, j in pairs], jnp.int32)
    use_bias = bias is not None
    in_specs = [pl.BlockSpec((heads, tq, dqk), lambda b, s, it, jt, fl: (qb, b * nq + it[s], 0)),
                pl.BlockSpec((heads, tk, dqk), lambda b, s, it, jt, fl: (kb, b * nk + jt[s], 0)),
                pl.BlockSpec((heads, tk, dv), lambda b, s, it, jt, fl: (0, b * nk + jt[s], 0))]
    args = [q, k, v]
    if use_bias:
        in_specs.append(pl.BlockSpec((1, 1, 1, tq, tk), lambda b, s, it, jt, fl: (b, it[s], jt[s], 0, 0)))
        args.append(bias)
    stat = pltpu.VMEM((heads, tq, LANES), F32)
    scratch = [stat, stat, stat, stat, pltpu.VMEM((2, tq, tk), F32), pltpu.VMEM((2, tq, tk), BF16),
               pltpu.VMEM((tq, tk), F32)]
    return pl.pallas_call(
        functools.partial(_attn_kernel, heads=heads, use_bias=use_bias),
        grid_spec=pltpu.PrefetchScalarGridSpec(
            num_scalar_prefetch=3, grid=(batch, len(pairs)), in_specs=in_specs,
            out_specs=pl.BlockSpec((tq, heads * dv), lambda b, s, it, jt, fl: (b * nq + it[s], 0)),
            scratch_shapes=scratch),
        out_shape=jax.ShapeDtypeStruct((batch * seq, heads * dv), BF16),
        compiler_params=_params("parallel", "arbitrary"))(it, jt, fl, *args)


def _residual_ln(x, upd, g, b, alpha):
    y = alpha * x + upd
    mu = jnp.mean(y, axis=-1, keepdims=True)
    d = y - mu
    var = jnp.mean(jnp.square(d), axis=-1, keepdims=True)
    return d * lax.rsqrt(var + LN_EPS) * g + b


def _outproj_ln_kernel(a_ref, b_ref, wa_ref, wb_ref, x_ref, g_ref, beta_ref, o_ref, ob_ref, *, alpha):
    for r in range(o_ref.shape[0] // LN_PANEL):
        rows = slice(r * LN_PANEL, (r + 1) * LN_PANEL)
        mix = jnp.dot(a_ref[rows, :], wa_ref[...], preferred_element_type=F32)
        mix = mix + jnp.dot(b_ref[rows, :], wb_ref[...], preferred_element_type=F32)
        y = _residual_ln(x_ref[rows, :], mix, g_ref[...], beta_ref[...], alpha)
        o_ref[rows, :] = y
        ob_ref[rows, :] = y.astype(BF16)


def _outproj_ln(a, b, w_o, layer, x, g, beta, alpha, tm=512):
    t, d = x.shape
    wa_rows = a.shape[1]
    row = lambda w: pl.BlockSpec((tm, w), lambda i: (i, 0))
    full = lambda arr: pl.BlockSpec(arr.shape, lambda i: (0, 0))
    half = lambda r: pl.BlockSpec((None, wa_rows, d), lambda i: (layer, r, 0))
    return pl.pallas_call(
        functools.partial(_outproj_ln_kernel, alpha=alpha), grid=(t // tm,),
        in_specs=[row(wa_rows), row(b.shape[1]), half(0), half(1), row(d), full(g), full(beta)],
        out_specs=[row(d), row(d)],
        out_shape=[jax.ShapeDtypeStruct((t, d), F32), jax.ShapeDtypeStruct((t, d), BF16)],
        compiler_params=_params("parallel"))(a, b, w_o, w_o, x, g, beta)


def _ffn_up_kernel(x_ref, wg_ref, wu_ref, h_ref):
    xb = x_ref[...]
    g = jnp.dot(xb, wg_ref[...].astype(BF16), preferred_element_type=F32)
    u = jnp.dot(xb, wu_ref[...].astype(BF16), preferred_element_type=F32)
    h_ref[...] = (g * (1.0 / (1.0 + jnp.exp(-g))) * u).astype(h_ref.dtype)


def _ffn_up(x, wg, wu, layer, tm=2048, tn=512):
    t, d = x.shape
    f = wg.shape[2]
    return pl.pallas_call(
        _ffn_up_kernel, grid=(t // tm, f // tn),
        in_specs=[pl.BlockSpec((tm, d), lambda i, j: (i, 0)),
                  pl.BlockSpec((None, d, tn), lambda i, j: (layer, 0, j)),
                  pl.BlockSpec((None, d, tn), lambda i, j: (layer, 0, j))],
        out_specs=pl.BlockSpec((tm, tn), lambda i, j: (i, j)),
        out_shape=jax.ShapeDtypeStruct((t, f), BF16),
        compiler_params=_params("parallel", "arbitrary"))(x, wg, wu)


def _ffn_down_ln_kernel(h_ref, w_ref, x_ref, g_ref, beta_ref, o_ref, *rest, alpha):
    ob_ref, acc_ref = rest if len(rest) == 2 else (None, rest[0])
    k = pl.program_id(1)

    @pl.when(k == 0)
    def _():
        acc_ref[...] = jnp.zeros(acc_ref.shape, F32)

    last = pl.num_programs(1) - 1

    @pl.when(k < last)
    def _():
        acc_ref[...] += jnp.dot(h_ref[...], w_ref[...], preferred_element_type=F32)

    @pl.when(k == last)
    def _():
        for r in range(o_ref.shape[0] // LN_PANEL):
            rows = slice(r * LN_PANEL, (r + 1) * LN_PANEL)
            tot = acc_ref[rows, :] + jnp.dot(h_ref[rows, :], w_ref[...], preferred_element_type=F32)
            y = _residual_ln(x_ref[rows, :], tot, g_ref[...], beta_ref[...], alpha)
            o_ref[rows, :] = y
            if ob_ref is not None:
                ob_ref[rows, :] = y.astype(BF16)


def _ffn_down_ln(h, w, layer, x, g, beta, alpha, with_copy, tm=512, tk=1408):
    t, d = x.shape
    f = h.shape[1]
    rows = lambda: pl.BlockSpec((tm, d), lambda i, k: (i, 0))
    return pl.pallas_call(
        functools.partial(_ffn_down_ln_kernel, alpha=alpha), grid=(t // tm, f // tk),
        in_specs=[pl.BlockSpec((tm, tk), lambda i, k: (i, k)),
                  pl.BlockSpec((None, tk, d), lambda i, k: (layer, k, 0)),
                  rows(),
                  pl.BlockSpec((1, d), lambda i, k: (0, 0)),
                  pl.BlockSpec((1, d), lambda i, k: (0, 0))],
        out_specs=[rows(), rows()][:2 if with_copy else 1],
        out_shape=[jax.ShapeDtypeStruct((t, d), F32), jax.ShapeDtypeStruct((t, d), BF16)][:2 if with_copy else 1],
        scratch_shapes=[pltpu.VMEM((tm, d), F32)],
        compiler_params=_params("parallel", "arbitrary"))(h, w, x, g, beta)


def _pack_small_weights(w_in, w_uq, w_ukv):
    sl = lambda k: w_in[:, _OFF[k] - _OFF[4]:_OFF[k + 1] - _OFF[4]]
    wi, ki, cq, ckv, kr = (sl(k) for k in range(4, 9))
    d = w_in.shape[0]
    zeros = lambda n: jnp.zeros((d, n), w_in.dtype)
    w_misc = jnp.concatenate([cq, ckv, kr, wi, zeros(LANES - MLA_ROPE - IDX_HEADS), ki, zeros(LANES - IDX_DIM)], axis=1)
    uq = w_uq.reshape(Q_LORA, MLA_HEADS, MLA_NOPE + MLA_ROPE)
    uq = jnp.pad(uq, ((0, 0), (0, 0), (0, MLA_QK_PAD - MLA_NOPE - MLA_ROPE))).reshape(Q_LORA, MLA_HEADS * MLA_QK_PAD)
    ukv = w_ukv.reshape(KV_LORA, MLA_HEADS, MLA_NOPE + MLA_V)
    uk = ukv[:, :, :MLA_NOPE].reshape(KV_LORA, MLA_HEADS * MLA_NOPE)
    uv = ukv[:, :, MLA_NOPE:].reshape(KV_LORA, MLA_HEADS * MLA_V)
    return w_misc, uq, uk, uv


def kernel(x, positions, w_in, g_cq, g_ckv, w_uq, w_ukv, w_o, ln1_g, ln1_b, w_gate, w_up, w_down, ln2_g, ln2_b):
    batch, seq, d = x.shape
    depth = w_in.shape[0]
    alpha = (2 * depth) ** 0.25
    topk = min(TOPK_MAX, seq // 4)
    tab_a = _rope_tables(positions, A_ROT_DIM, ROPE_THETA, A_HEAD_DIM)
    tab_i = _rope_tables(positions, IDX_ROT_DIM, ROPE_THETA, IDX_DIM)
    tab_m = _rope_tables(positions, MLA_ROPE, MLA_ROPE_THETA, LANES)
    xt = x.reshape(batch * seq, d)
    xb = xt
    w_tail_b = w_in[:, :, _OFF[4]:].astype(BF16)
    w_uq_b, w_ukv_b, w_o_b, w_down_b = (w.astype(BF16) for w in (w_uq, w_ukv, w_o, w_down))
    tile = A_WIDTH
    for l in range(depth):
        w_misc, uq, uk, uv = _pack_small_weights(w_tail_b[l], w_uq_b[l], w_ukv_b[l])
        qk = _project(xb, w_in, BF16, tab_a, A_ROT_DIM // 2, scale0=A_QSCALE, head_major=True,
                      layer=l, col0=0, n=2 * tile, tn=tile)
        va = _project(xb, w_in, BF16, head_major=True, layer=l, col0=2, n=tile, tn=tile)
        qi = _project(xb, w_in, BF16, tab_i, IDX_ROT_DIM // 2, layer=l, col0=3, n=tile, tn=tile)
        misc = _project(xb, w_misc, F32)
        q_mla, k_mla, v_mla, ki2, wi = _mla_prep(
            misc, g_cq[l].reshape(1, -1), g_ckv[l].reshape(1, -1), uq, uk, uv, tab_m, tab_i)
        bias = _indexer(qi, wi.T, ki2, batch, seq, topk)
        out_a = _attention(qk, qk, va, bias, batch, seq, q_head0=0, k_head0=A_HEADS, heads=A_HEADS)
        out_b = _attention(q_mla, k_mla, v_mla, None, batch, seq, heads=MLA_HEADS)
        xt, xb = _outproj_ln(out_a, out_b, w_o_b, l, xt,
                             ln1_g[l].reshape(1, -1), ln1_b[l].reshape(1, -1), alpha)
        h = _ffn_up(xb, w_gate, w_up, l)
        outs = _ffn_down_ln(h, w_down_b, l, xt, ln2_g[l].reshape(1, -1), ln2_b[l].reshape(1, -1), alpha,
                            with_copy=l + 1 < depth)
        xt, xb = outs if l + 1 < depth else (outs[0], None)
    return xt.reshape(batch, seq, d)
```

```python
import functools
import math

import numpy as np
import jax
import jax.numpy as jnp
from jax import lax
from jax.experimental import pallas as pl
from jax.experimental.pallas import tpu as pltpu

F32 = jnp.float32
BF16 = jnp.bfloat16

D_MODEL = 2048
A_HEADS = 8
A_HEAD_DIM = 128
A_ROT_DIM = A_HEAD_DIM // 4
IDX_HEADS = 16
IDX_DIM = 64
IDX_ROT_DIM = IDX_DIM // 4
TOPK_MAX = 256
MLA_HEADS = 8
MLA_NOPE = 128
MLA_ROPE = 64
MLA_V = 128
MLA_QK_PAD = 256
Q_LORA = 512
KV_LORA = 256
FFN_DIM = 5632
ROPE_THETA = 500000.0
MLA_ROPE_THETA = 10000.0
LN_EPS = 1e-5
RMS_EPS = 1e-6
A_WIDTH = A_HEADS * A_HEAD_DIM
LANES = 128
SUBLANES = 8
MASK_NEG = -1e30
VMEM_LIMIT = 56 * 1024 * 1024
LOG2E = math.log2(math.e)
A_QSCALE = A_HEAD_DIM ** -0.5 * LOG2E
MLA_QSCALE = (MLA_NOPE + MLA_ROPE) ** -0.5 * LOG2E
LN_PANEL = 128
FFN_PANEL = 256
ATTN_TQ = 512
ATTN_TK = 512

_OFF = np.cumsum([0, A_WIDTH, A_WIDTH, A_WIDTH, IDX_HEADS * IDX_DIM, IDX_HEADS, IDX_DIM,
                  Q_LORA, KV_LORA, MLA_ROPE]).tolist()
MISC_W = 1024
MISC_CQ = 0
MISC_CKV = Q_LORA
MISC_KR = Q_LORA + KV_LORA
MISC_KI = MISC_KR + LANES


def _params(*sem):
    return pltpu.CompilerParams(dimension_semantics=sem, vmem_limit_bytes=VMEM_LIMIT)


def _rope_tables(positions, rot_dim, theta, period):
    half = rot_dim // 2
    inv_freq = theta ** (-2.0 * jnp.arange(half, dtype=F32) / rot_dim)
    ang = inv_freq[:, None] * positions.astype(F32).reshape(1, -1)
    cos, sin = lax.optimization_barrier((jnp.cos(ang), jnp.sin(ang)))
    t = cos.shape[1]
    rest = period - rot_dim
    z_h = jnp.zeros((half, t), F32)
    z_r = jnp.zeros((rest, t), F32)
    c = jnp.concatenate([cos, cos, jnp.ones((rest, t), F32)], axis=0)
    s1 = jnp.concatenate([-sin, z_h, z_r], axis=0)
    s2 = jnp.concatenate([z_h, sin, z_r], axis=0)
    rep = LANES // period
    return tuple(jnp.tile(a, (rep, 1)).T for a in (c, s1, s2))


def _rope_chunk(a, c, s1, s2, half):
    return a * c + pltpu.roll(a, LANES - half, 1) * s1 + pltpu.roll(a, half, 1) * s2


def _store_cols(o_ref, k, val):
    if len(o_ref.shape) == 3:
        o_ref[k] = val.astype(o_ref.dtype)
    else:
        o_ref[:, k * LANES:(k + 1) * LANES] = val.astype(o_ref.dtype)


def _mm_kernel(x_ref, w_ref, o_ref):
    acc = jnp.dot(x_ref[...].astype(BF16), w_ref[...].astype(BF16), preferred_element_type=F32)
    if len(o_ref.shape) == 3:
        for k in range(acc.shape[1] // LANES):
            _store_cols(o_ref, k, acc[:, k * LANES:(k + 1) * LANES])
    else:
        o_ref[...] = acc.astype(o_ref.dtype)


def _mm_rope_kernel(x_ref, w_ref, c_ref, s1_ref, s2_ref, o_ref, *, half, scale0):
    acc = jnp.dot(x_ref[...].astype(BF16), w_ref[...].astype(BF16), preferred_element_type=F32)
    if scale0 is not None:
        acc = acc * jnp.where(pl.program_id(1) == 0, scale0, 1.0)
    c, s1, s2 = c_ref[...], s1_ref[...], s2_ref[...]
    for k in range(acc.shape[1] // LANES):
        _store_cols(o_ref, k, _rope_chunk(acc[:, k * LANES:(k + 1) * LANES], c, s1, s2, half))


def _project(x, w, out_dtype, tables=None, half=0, scale0=None, head_major=False, layer=0, col0=0, n=None,
             tm=1024, tn=1024):
    t, kdim = x.shape
    n = w.shape[-1] if n is None else n
    grid = (t // tm, n // tn)
    x_spec = pl.BlockSpec((tm, kdim), lambda i, j: (i, 0))
    if w.ndim == 3:
        w_spec = pl.BlockSpec((None, kdim, tn), lambda i, j: (layer, 0, col0 + j))
    else:
        w_spec = pl.BlockSpec((kdim, tn), lambda i, j: (0, j))
    if head_major:
        o_spec = pl.BlockSpec((tn // LANES, tm, LANES), lambda i, j: (j, i, 0))
        o_shape = jax.ShapeDtypeStruct((n // LANES, t, LANES), out_dtype)
    else:
        o_spec = pl.BlockSpec((tm, tn), lambda i, j: (i, j))
        o_shape = jax.ShapeDtypeStruct((t, n), out_dtype)
    if tables is None:
        return pl.pallas_call(
            _mm_kernel, grid=grid, in_specs=[x_spec, w_spec], out_specs=o_spec, out_shape=o_shape,
            compiler_params=_params("parallel", "arbitrary"))(x, w)
    t_spec = pl.BlockSpec((tm, LANES), lambda i, j: (i, 0))
    return pl.pallas_call(
        functools.partial(_mm_rope_kernel, half=half, scale0=scale0), grid=grid,
        in_specs=[x_spec, w_spec, t_spec, t_spec, t_spec], out_specs=o_spec, out_shape=o_shape,
        compiler_params=_params("parallel", "arbitrary"))(x, w, *tables)


def _mla_prep_kernel(misc_ref, gq_ref, gkv_ref, wq_ref, wk_ref, wv_ref,
                     cm_ref, s1m_ref, s2m_ref, ci_ref, s1i_ref, s2i_ref,
                     q_ref, k_ref, v_ref, ki_ref, wi_ref):
    tm = misc_ref.shape[0]
    cm, s1m, s2m = cm_ref[...], s1m_ref[...], s2m_ref[...]
    lane = lax.broadcasted_iota(jnp.int32, (tm, LANES), 1)

    cq = misc_ref[:, MISC_CQ:MISC_CQ + Q_LORA]
    cqn = cq * lax.rsqrt(jnp.mean(jnp.square(cq), axis=-1, keepdims=True) + RMS_EPS) * gq_ref[...]
    q_all = jnp.dot(cqn.astype(BF16), wq_ref[...], preferred_element_type=F32) * MLA_QSCALE
    for h in range(MLA_HEADS):
        base = h * MLA_QK_PAD
        q_ref[h, :, 0:MLA_NOPE] = q_all[:, base:base + MLA_NOPE].astype(BF16)
        pe = q_all[:, base + MLA_NOPE:base + MLA_QK_PAD]
        q_ref[h, :, MLA_NOPE:MLA_QK_PAD] = _rope_chunk(pe, cm, s1m, s2m, MLA_ROPE // 2).astype(BF16)

    ckv = misc_ref[:, MISC_CKV:MISC_CKV + KV_LORA]
    ckvn = (ckv * lax.rsqrt(jnp.mean(jnp.square(ckv), axis=-1, keepdims=True) + RMS_EPS) * gkv_ref[...]).astype(BF16)
    kn = jnp.dot(ckvn, wk_ref[...], preferred_element_type=F32)
    vn = jnp.dot(ckvn, wv_ref[...], preferred_element_type=F32)

    krc = misc_ref[:, MISC_KR:MISC_KR + LANES]
    kpe = jnp.where(lane < MLA_ROPE, _rope_chunk(krc, cm, s1m, s2m, MLA_ROPE // 2), 0.0).astype(BF16)
    for h in range(MLA_HEADS):
        k_ref[h, :, 0:MLA_NOPE] = kn[:, h * MLA_NOPE:(h + 1) * MLA_NOPE].astype(BF16)
        k_ref[h, :, MLA_NOPE:MLA_QK_PAD] = kpe
        v_ref[h] = vn[:, h * MLA_V:(h + 1) * MLA_V].astype(BF16)

    kic = misc_ref[:, MISC_KI:MISC_KI + LANES]
    ki_lo = _rope_chunk(kic, ci_ref[...], s1i_ref[...], s2i_ref[...], IDX_ROT_DIM // 2)
    ki_ref[:, 0:LANES] = ki_lo.astype(BF16)
    ki_ref[:, LANES:2 * LANES] = pltpu.roll(ki_lo, IDX_DIM, 1).astype(BF16)
    wi_ref[...] = krc[:, MLA_ROPE:MLA_ROPE + IDX_HEADS] * (IDX_HEADS ** -0.5 * IDX_DIM ** -0.5)


def _mla_prep(misc, gq, gkv, wq, wk, wv, tab_m, tab_i, tm=512):
    t = misc.shape[0]
    row = lambda w: pl.BlockSpec((tm, w), lambda i: (i, 0))
    full = lambda a: pl.BlockSpec(a.shape, lambda i: (0, 0))
    heads = lambda w: pl.BlockSpec((MLA_HEADS, tm, w), lambda i: (0, i, 0))
    return pl.pallas_call(
        _mla_prep_kernel, grid=(t // tm,),
        in_specs=[row(MISC_W), full(gq), full(gkv), full(wq), full(wk), full(wv)] + [row(LANES)] * 6,
        out_specs=[heads(MLA_QK_PAD), heads(MLA_QK_PAD), heads(MLA_V), row(2 * LANES), row(IDX_HEADS)],
        out_shape=[jax.ShapeDtypeStruct((MLA_HEADS, t, MLA_QK_PAD), BF16),
                   jax.ShapeDtypeStruct((MLA_HEADS, t, MLA_QK_PAD), BF16),
                   jax.ShapeDtypeStruct((MLA_HEADS, t, MLA_V), BF16),
                   jax.ShapeDtypeStruct((t, 2 * LANES), BF16),
                   jax.ShapeDtypeStruct((t, IDX_HEADS), F32)],
        compiler_params=_params("parallel"))(misc, gq, gkv, wq, wk, wv, *tab_m, *tab_i)


def _indexer_kernel(q_ref, w_ref, k_ref, bias_ref, sc_ref, *, tq, tk, topk, max_iter):
    i = pl.program_id(1)
    nk = bias_ref.shape[2]
    nb = (i * tq + tq - 1) // tk + 1
    kf = float(topk)
    qpos = i * tq + lax.broadcasted_iota(jnp.int32, (1, tq), 1)
    kpos0 = lax.broadcasted_iota(jnp.int32, (tk, 1), 0)
    nt = (((1,), (1,)), ((), ()))
    groups = tk // SUBLANES
    fold_rows = 4 * SUBLANES

    def fold_sum(x):
        return jnp.sum(x.reshape(tk // fold_rows, fold_rows, tq), axis=0)

    def score_blk(j, carry):
        mx, mn = carry
        start = pl.multiple_of(j * tk, tk)
        k_lo = k_ref[pl.ds(start, tk), 0:LANES]
        k_hi = k_ref[pl.ds(start, tk), LANES:2 * LANES]
        acc = jnp.zeros((tk, tq), F32)
        for p in range(IDX_HEADS // 2):
            qp = q_ref[:, p * LANES:(p + 1) * LANES]
            s0 = lax.dot_general(k_lo, qp, nt, preferred_element_type=F32)
            s1 = lax.dot_general(k_hi, qp, nt, preferred_element_type=F32)
            acc = acc + jnp.maximum(s0, 0.0) * w_ref[2 * p:2 * p + 1, :]
            acc = acc + jnp.maximum(s1, 0.0) * w_ref[2 * p + 1:2 * p + 2, :]
        causal = (j * tk + kpos0) <= qpos
        acc = acc + 0.0
        sc_ref[j] = jnp.where(causal, acc, -jnp.inf)
        mx = jnp.maximum(mx, jnp.max(jnp.where(causal, acc, -jnp.inf).reshape(groups, SUBLANES, tq), axis=0))
        mn = jnp.minimum(mn, jnp.min(jnp.where(causal, acc, jnp.inf).reshape(groups, SUBLANES, tq), axis=0))
        return mx, mn

    mx8, mn8 = lax.fori_loop(0, nb, score_blk, (jnp.full((SUBLANES, tq), -jnp.inf, F32),
                                                 jnp.full((SUBLANES, tq), jnp.inf, F32)))
    rmax = jnp.max(mx8, axis=0, keepdims=True)
    rmin = jnp.min(mn8, axis=0, keepdims=True)

    def count_ge(th):
        def body(j, c):
            for r in range(tk // fold_rows):
                c = c + jnp.where(sc_ref[j, r * fold_rows:(r + 1) * fold_rows, :] >= th, 1.0, 0.0)
            return c
        part = lax.fori_loop(0, nb, body, jnp.zeros((fold_rows, tq), F32))
        return jnp.sum(part, axis=0, keepdims=True)

    c_all = (qpos + 1).astype(F32)
    c_top = count_ge(rmax)
    top_full = c_top >= kf
    lo = jnp.where(top_full, rmax, rmin)
    c_lo = jnp.where(top_full, c_top, c_all)
    hi = jnp.where(top_full, jnp.inf, rmax)
    c_hi = jnp.where(top_full, 0.0, c_top)
    done = jnp.where(top_full | (c_lo <= kf), 1.0, 0.0)

    def cond(st):
        it, _, _, _, _, dn = st
        return jnp.logical_and(it < max_iter, jnp.min(dn) < 0.5)

    def halve(st):
        it, lo, hi, c_lo, c_hi, dn = st
        th = 0.5 * lo + 0.5 * hi
        stuck = (th <= lo) | (th >= hi)
        c = count_ge(th)
        active = jnp.logical_not(stuck) & (dn < 0.5)
        up = (c >= kf) & active
        down = (c < kf) & active
        lo = jnp.where(up, th, lo)
        c_lo = jnp.where(up, c, c_lo)
        hi = jnp.where(down, th, hi)
        c_hi = jnp.where(down, c, c_hi)
        dn = jnp.where(stuck | (c_lo <= kf), 1.0, dn)
        return it + 1, lo, hi, c_lo, c_hi, dn

    def step(st):
        return halve(halve(st))

    _, lo, hi, c_lo, c_hi, _ = lax.while_loop(cond, step, (jnp.int32(0), lo, hi, c_lo, c_hi, done))

    need = kf - c_hi
    last_key = jnp.full((1, tq), nk * tk - 1, jnp.int32)

    def band_prefix(kcut):
        def body(j, c):
            s = sc_ref[j]
            hit = (s >= lo) & (s < hi) & ((j * tk + kpos0) < kcut)
            return c + fold_sum(jnp.where(hit, 1.0, 0.0))
        part = lax.fori_loop(0, nb, body, jnp.zeros((fold_rows, tq), F32))
        return jnp.sum(part, axis=0, keepdims=True)

    def tie_search(_):
        nbits = int(nk * tk - 1).bit_length()
        kcut = jnp.zeros((1, tq), jnp.int32)
        for b in range(nbits - 1, -1, -1):
            trial = kcut | (1 << b)
            kcut = jnp.where(band_prefix(trial) < need, trial, kcut)
        return kcut

    has_excess = jnp.max(c_lo) > kf
    kcut = lax.cond(has_excess, tie_search, lambda _: last_key, 0)
    kcut = jnp.where(c_lo > kf, kcut, last_key)

    def emit(j, carry):
        s = sc_ref[j]
        sel = (s >= lo) & ((s >= hi) | ((j * tk + kpos0) <= kcut))
        bias_ref[0, 0, j] = jnp.where(sel, 0.0, MASK_NEG).T.astype(bias_ref.dtype)
        return carry

    lax.fori_loop(0, nb, emit, 0)

    def fill(j, carry):
        bias_ref[0, 0, j] = jnp.full((tq, tk), MASK_NEG, bias_ref.dtype)
        return carry

    lax.fori_loop(nb, nk, fill, 0)


def _indexer(qi, wi_t, ki2, batch, seq, topk, tq=ATTN_TQ, tk=ATTN_TK):
    nq, nk = seq // tq, seq // tk
    kern = functools.partial(_indexer_kernel, tq=tq, tk=tk, topk=topk, max_iter=128)
    return pl.pallas_call(
        kern, grid=(batch, nq),
        in_specs=[pl.BlockSpec((tq, IDX_HEADS * IDX_DIM), lambda b, i: (b * nq + i, 0)),
                  pl.BlockSpec((IDX_HEADS, tq), lambda b, i: (0, b * nq + i)),
                  pl.BlockSpec((seq, 2 * LANES), lambda b, i: (b, 0))],
        out_specs=pl.BlockSpec((1, 1, nk, tq, tk), lambda b, i: (b, i, 0, 0, 0)),
        out_shape=jax.ShapeDtypeStruct((batch, nq, nk, tq, tk), BF16),
        scratch_shapes=[pltpu.VMEM((nk, tk, tq), F32)],
        compiler_params=_params("parallel", "arbitrary"))(qi, wi_t, ki2)


def _attn_kernel(it_ref, jt_ref, fl_ref, q_ref, k_ref, v_ref, *rest, heads, use_bias):
    if use_bias:
        bias_ref, o_ref, m_ref, l_ref, al_ref, acc_ref, s_ref, p_ref, bf_ref = rest
    else:
        o_ref, m_ref, l_ref, al_ref, acc_ref, s_ref, p_ref, bf_ref = rest
    step = pl.program_id(1)
    i = it_ref[step]
    j = jt_ref[step]
    flags = fl_ref[step]
    tq = q_ref.shape[1]
    tk = k_ref.shape[1]
    nt = (((1,), (1,)), ((), ()))

    @pl.when(j == 0)
    def _():
        m_ref[...] = jnp.full(m_ref.shape, -jnp.inf, F32)
        l_ref[...] = jnp.zeros(l_ref.shape, F32)
        acc_ref[...] = jnp.zeros(acc_ref.shape, F32)

    nch = tk // LANES

    def logits(h, slot, biased):
        s = lax.dot_general(q_ref[h], k_ref[h], nt, preferred_element_type=F32)
        if biased:
            s = s + bf_ref[...]
        s_ref[slot] = s

    def softmax(h, slot):
        mc = s_ref[slot, :, 0:LANES]
        for c in range(1, nch):
            mc = jnp.maximum(mc, s_ref[slot, :, c * LANES:(c + 1) * LANES])
        m_prev = m_ref[h]
        m_new = jnp.maximum(m_prev, jnp.max(mc, axis=1, keepdims=True))
        alpha = jnp.exp2(m_prev - m_new)
        m_ref[h] = m_new
        al_ref[h] = alpha
        lsum = None
        for c in range(nch):
            cols = slice(c * LANES, (c + 1) * LANES)
            pc = jnp.exp2(s_ref[slot, :, cols] - m_new)
            lsum = pc if lsum is None else lsum + pc
            p_ref[slot, :, cols] = pc.astype(BF16)
        l_ref[h] = alpha * l_ref[h] + jnp.sum(lsum, axis=1, keepdims=True)

    def weighted_values(h, slot):
        pv = jnp.dot(p_ref[slot], v_ref[h], preferred_element_type=F32)
        acc_ref[h] = al_ref[h] * acc_ref[h] + pv

    def body(biased):
        logits(0, 0, biased)
        logits(1, 1, biased)
        softmax(0, 0)

        for h in range(1, heads - 1):
            cur = h % 2
            logits(h + 1, 1 - cur, biased)
            softmax(h, cur)
            weighted_values(h - 1, 1 - cur)
        last = (heads - 1) % 2
        softmax(heads - 1, last)
        weighted_values(heads - 2, 1 - last)
        weighted_values(heads - 1, last)

    if use_bias:
        bf_ref[...] = bias_ref[0, 0, 0].astype(F32)
        body(True)
    else:
        diag = (flags & 2) != 0

        @pl.when(diag)
        def _():
            row = i * tq + lax.broadcasted_iota(jnp.int32, (tq, 1), 0)
            col = j * tk + lax.broadcasted_iota(jnp.int32, (1, tk), 1)
            bf_ref[...] = jnp.where(col <= row, 0.0, MASK_NEG)
            body(True)

        pl.when(jnp.logical_not(diag))(lambda: body(False))

    @pl.when((flags & 1) != 0)
    def _():
        for h in range(heads):
            o_ref[:, h * LANES:(h + 1) * LANES] = (acc_ref[h] * (1.0 / l_ref[h])).astype(o_ref.dtype)


def _attention(q, k, v, bias, batch, seq, q_head0=0, k_head0=0, heads=8, tq=ATTN_TQ, tk=ATTN_TK):
    dqk, dv = q.shape[2], v.shape[2]
    assert dv == LANES and q_head0 % heads == 0 and k_head0 % heads == 0
    qb, kb = q_head0 // heads, k_head0 // heads
    nq, nk = seq // tq, seq // tk
    pairs = [(i, j) for i in range(nq) for j in range((i * tq + tq - 1) // tk + 1)]
    it = jnp.asarray([p[0] for p in pairs], jnp.int32)
    jt = jnp.asarray([p[1] for p in pairs], jnp.int32)
    fl = jnp.asarray([(1 if j == (i * tq + tq - 1) // tk else 0) + (2 if (j + 1) * tk - 1 > i * tq else 0)
                      for i, j in pairs], jnp.int32)
    use_bias = bias is not None
    in_specs = [pl.BlockSpec((heads, tq, dqk), lambda b, s, it, jt, fl: (qb, b * nq + it[s], 0)),
                pl.BlockSpec((heads, tk, dqk), lambda b, s, it, jt, fl: (kb, b * nk + jt[s], 0)),
                pl.BlockSpec((heads, tk, dv), lambda b, s, it, jt, fl: (0, b * nk + jt[s], 0))]
    args = [q, k, v]
    if use_bias:
        in_specs.append(pl.BlockSpec((1, 1, 1, tq, tk), lambda b, s, it, jt, fl: (b, it[s], jt[s], 0, 0)))
        args.append(bias)
    stat = pltpu.VMEM((heads, tq, LANES), F32)
    scratch = [stat, stat, stat, stat, pltpu.VMEM((2, tq, tk), F32), pltpu.VMEM((2, tq, tk), BF16),
               pltpu.VMEM((tq, tk), F32)]
    return pl.pallas_call(
        functools.partial(_attn_kernel, heads=heads, use_bias=use_bias),
        grid_spec=pltpu.PrefetchScalarGridSpec(
            num_scalar_prefetch=3, grid=(batch, len(pairs)), in_specs=in_specs,
            out_specs=pl.BlockSpec((tq, heads * dv), lambda b, s, it, jt, fl: (b * nq + it[s], 0)),
            scratch_shapes=scratch),
        out_shape=jax.ShapeDtypeStruct((batch * seq, heads * dv), BF16),
        compiler_params=_params("parallel", "arbitrary"))(it, jt, fl, *args)


def _residual_ln(x, upd, g, b, alpha):
    y = alpha * x + upd
    mu = jnp.mean(y, axis=-1, keepdims=True)
    d = y - mu
    var = jnp.mean(jnp.square(d), axis=-1, keepdims=True)
    return d * lax.rsqrt(var + LN_EPS) * g + b


def _outproj_ln_kernel(a_ref, b_ref, wa_ref, wb_ref, x_ref, g_ref, beta_ref, o_ref, ob_ref, *, alpha):
    for r in range(o_ref.shape[0] // LN_PANEL):
        rows = slice(r * LN_PANEL, (r + 1) * LN_PANEL)
        mix = jnp.dot(a_ref[rows, :], wa_ref[...], preferred_element_type=F32)
        mix = mix + jnp.dot(b_ref[rows, :], wb_ref[...], preferred_element_type=F32)
        y = _residual_ln(x_ref[rows, :], mix, g_ref[...], beta_ref[...], alpha)
        o_ref[rows, :] = y
        ob_ref[rows, :] = y.astype(BF16)


def _outproj_ln(a, b, w_o, layer, x, g, beta, alpha, tm=512):
    t, d = x.shape
    wa_rows = a.shape[1]
    row = lambda w: pl.BlockSpec((tm, w), lambda i: (i, 0))
    full = lambda arr: pl.BlockSpec(arr.shape, lambda i: (0, 0))
    half = lambda r: pl.BlockSpec((None, wa_rows, d), lambda i: (layer, r, 0))
    return pl.pallas_call(
        functools.partial(_outproj_ln_kernel, alpha=alpha), grid=(t // tm,),
        in_specs=[row(wa_rows), row(b.shape[1]), half(0), half(1), row(d), full(g), full(beta)],
        out_specs=[row(d), row(d)],
        out_shape=[jax.ShapeDtypeStruct((t, d), F32), jax.ShapeDtypeStruct((t, d), BF16)],
        compiler_params=_params("parallel"))(a, b, w_o, w_o, x, g, beta)


def _ffn_up_kernel(x_ref, wg_ref, wu_ref, h_ref):
    xb = x_ref[...]
    for p in range(h_ref.shape[1] // FFN_PANEL):
        cols = slice(p * FFN_PANEL, (p + 1) * FFN_PANEL)
        g = jnp.dot(xb, wg_ref[:, cols].astype(BF16), preferred_element_type=F32)
        u = jnp.dot(xb, wu_ref[:, cols].astype(BF16), preferred_element_type=F32)
        h_ref[:, cols] = (g * (1.0 / (1.0 + jnp.exp(-g))) * u).astype(h_ref.dtype)


def _ffn_up(x, wg, wu, layer, tm=2048, tn=512):
    t, d = x.shape
    f = wg.shape[2]
    return pl.pallas_call(
        _ffn_up_kernel, grid=(t // tm, f // tn),
        in_specs=[pl.BlockSpec((tm, d), lambda i, j: (i, 0)),
                  pl.BlockSpec((None, d, tn), lambda i, j: (layer, 0, j)),
                  pl.BlockSpec((None, d, tn), lambda i, j: (layer, 0, j))],
        out_specs=pl.BlockSpec((tm, tn), lambda i, j: (i, j)),
        out_shape=jax.ShapeDtypeStruct((t, f), BF16),
        compiler_params=_params("parallel", "arbitrary"))(x, wg, wu)


def _ffn_down_ln_kernel(h_ref, w_ref, x_ref, g_ref, beta_ref, o_ref, *rest, alpha):
    ob_ref, acc_ref = rest if len(rest) == 2 else (None, rest[0])
    k = pl.program_id(1)

    @pl.when(k == 0)
    def _():
        acc_ref[...] = jnp.zeros(acc_ref.shape, F32)

    acc_ref[...] += jnp.dot(h_ref[...], w_ref[...], preferred_element_type=F32)

    @pl.when(k == pl.num_programs(1) - 1)
    def _():
        y = _residual_ln(x_ref[...], acc_ref[...], g_ref[...], beta_ref[...], alpha)
        o_ref[...] = y
        if ob_ref is not None:
            ob_ref[...] = y.astype(BF16)


def _ffn_down_ln(h, w, layer, x, g, beta, alpha, with_copy, tm=512, tk=1408):
    t, d = x.shape
    f = h.shape[1]
    rows = lambda: pl.BlockSpec((tm, d), lambda i, k: (i, 0))
    return pl.pallas_call(
        functools.partial(_ffn_down_ln_kernel, alpha=alpha), grid=(t // tm, f // tk),
        in_specs=[pl.BlockSpec((tm, tk), lambda i, k: (i, k)),
                  pl.BlockSpec((None, tk, d), lambda i, k: (layer, k, 0)),
                  rows(),
                  pl.BlockSpec((1, d), lambda i, k: (0, 0)),
                  pl.BlockSpec((1, d), lambda i, k: (0, 0))],
        out_specs=[rows(), rows()][:2 if with_copy else 1],
        out_shape=[jax.ShapeDtypeStruct((t, d), F32), jax.ShapeDtypeStruct((t, d), BF16)][:2 if with_copy else 1],
        scratch_shapes=[pltpu.VMEM((tm, d), F32)],
        compiler_params=_params("parallel", "arbitrary"))(h, w, x, g, beta)


def _pack_small_weights(w_in, w_uq, w_ukv):
    sl = lambda k: w_in[:, _OFF[k] - _OFF[4]:_OFF[k + 1] - _OFF[4]]
    wi, ki, cq, ckv, kr = (sl(k) for k in range(4, 9))
    d = w_in.shape[0]
    zeros = lambda n: jnp.zeros((d, n), w_in.dtype)
    w_misc = jnp.concatenate([cq, ckv, kr, wi, zeros(LANES - MLA_ROPE - IDX_HEADS), ki, zeros(LANES - IDX_DIM)], axis=1)
    uq = w_uq.reshape(Q_LORA, MLA_HEADS, MLA_NOPE + MLA_ROPE)
    uq = jnp.pad(uq, ((0, 0), (0, 0), (0, MLA_QK_PAD - MLA_NOPE - MLA_ROPE))).reshape(Q_LORA, MLA_HEADS * MLA_QK_PAD)
    ukv = w_ukv.reshape(KV_LORA, MLA_HEADS, MLA_NOPE + MLA_V)
    uk = ukv[:, :, :MLA_NOPE].reshape(KV_LORA, MLA_HEADS * MLA_NOPE)
    uv = ukv[:, :, MLA_NOPE:].reshape(KV_LORA, MLA_HEADS * MLA_V)
    return w_misc, uq, uk, uv


def kernel(x, positions, w_in, g_cq, g_ckv, w_uq, w_ukv, w_o, ln1_g, ln1_b, w_gate, w_up, w_down, ln2_g, ln2_b):
    batch, seq, d = x.shape
    depth = w_in.shape[0]
    alpha = (2 * depth) ** 0.25
    topk = min(TOPK_MAX, seq // 4)
    tab_a = _rope_tables(positions, A_ROT_DIM, ROPE_THETA, A_HEAD_DIM)
    tab_i = _rope_tables(positions, IDX_ROT_DIM, ROPE_THETA, IDX_DIM)
    tab_m = _rope_tables(positions, MLA_ROPE, MLA_ROPE_THETA, LANES)
    xt = x.reshape(batch * seq, d)
    xb = xt
    w_tail_b = w_in[:, :, _OFF[4]:].astype(BF16)
    w_uq_b, w_ukv_b, w_o_b, w_down_b = (w.astype(BF16) for w in (w_uq, w_ukv, w_o, w_down))
    tile = A_WIDTH
    for l in range(depth):
        w_misc, uq, uk, uv = _pack_small_weights(w_tail_b[l], w_uq_b[l], w_ukv_b[l])
        qk = _project(xb, w_in, BF16, tab_a, A_ROT_DIM // 2, scale0=A_QSCALE, head_major=True,
                      layer=l, col0=0, n=2 * tile, tn=tile)
        va = _project(xb, w_in, BF16, head_major=True, layer=l, col0=2, n=tile, tn=tile)
        qi = _project(xb, w_in, BF16, tab_i, IDX_ROT_DIM // 2, layer=l, col0=3, n=tile, tn=tile)
        misc = _project(xb, w_misc, F32)
        q_mla, k_mla, v_mla, ki2, wi = _mla_prep(
            misc, g_cq[l].reshape(1, -1), g_ckv[l].reshape(1, -1), uq, uk, uv, tab_m, tab_i)
        bias = _indexer(qi, wi.T, ki2, batch, seq, topk)
        out_a = _attention(qk, qk, va, bias, batch, seq, q_head0=0, k_head0=A_HEADS, heads=A_HEADS)
        out_b = _attention(q_mla, k_mla, v_mla, None, batch, seq, heads=MLA_HEADS)
        xt, xb = _outproj_ln(out_a, out_b, w_o_b, l, xt,
                             ln1_g[l].reshape(1, -1), ln1_b[l].reshape(1, -1), alpha)
        h = _ffn_up(xb, w_gate, w_up, l)
        outs = _ffn_down_ln(h, w_down_b, l, xt, ln2_g[l].reshape(1, -1), ln2_b[l].reshape(1, -1), alpha,
                            with_copy=l + 1 < depth)
        xt, xb = outs if l + 1 < depth else (outs[0], None)
    return xt.reshape(batch, seq, d)
```

```python
import functools
import math

import numpy as np
import jax
import jax.numpy as jnp
from jax import lax
from jax.experimental import pallas as pl
from jax.experimental.pallas import tpu as pltpu

F32 = jnp.float32
BF16 = jnp.bfloat16

D_MODEL = 2048
A_HEADS = 8
A_HEAD_DIM = 128
A_ROT_DIM = A_HEAD_DIM // 4
IDX_HEADS = 16
IDX_DIM = 64
IDX_ROT_DIM = IDX_DIM // 4
TOPK_MAX = 256
MLA_HEADS = 8
MLA_NOPE = 128
MLA_ROPE = 64
MLA_V = 128
MLA_QK_PAD = 256
Q_LORA = 512
KV_LORA = 256
FFN_DIM = 5632
ROPE_THETA = 500000.0
MLA_ROPE_THETA = 10000.0
LN_EPS = 1e-5
RMS_EPS = 1e-6
A_WIDTH = A_HEADS * A_HEAD_DIM
LANES = 128
SUBLANES = 8
MASK_NEG = -1e30
VMEM_LIMIT = 56 * 1024 * 1024
LOG2E = math.log2(math.e)
A_QSCALE = A_HEAD_DIM ** -0.5 * LOG2E
MLA_QSCALE = (MLA_NOPE + MLA_ROPE) ** -0.5 * LOG2E
LN_PANEL = 128
FFN_PANEL = 256
ATTN_TQ = 512
ATTN_TK = 512

_OFF = np.cumsum([0, A_WIDTH, A_WIDTH, A_WIDTH, IDX_HEADS * IDX_DIM, IDX_HEADS, IDX_DIM,
                  Q_LORA, KV_LORA, MLA_ROPE]).tolist()
MISC_W = 1024
MISC_CQ = 0
MISC_CKV = Q_LORA
MISC_KR = Q_LORA + KV_LORA
MISC_KI = MISC_KR + LANES


def _params(*sem):
    return pltpu.CompilerParams(dimension_semantics=sem, vmem_limit_bytes=VMEM_LIMIT)


def _rope_tables(positions, rot_dim, theta, period):
    half = rot_dim // 2
    inv_freq = theta ** (-2.0 * jnp.arange(half, dtype=F32) / rot_dim)
    ang = inv_freq[:, None] * positions.astype(F32).reshape(1, -1)
    cos, sin = lax.optimization_barrier((jnp.cos(ang), jnp.sin(ang)))
    t = cos.shape[1]
    rest = period - rot_dim
    z_h = jnp.zeros((half, t), F32)
    z_r = jnp.zeros((rest, t), F32)
    c = jnp.concatenate([cos, cos, jnp.ones((rest, t), F32)], axis=0)
    s1 = jnp.concatenate([-sin, z_h, z_r], axis=0)
    s2 = jnp.concatenate([z_h, sin, z_r], axis=0)
    rep = LANES // period
    return tuple(jnp.tile(a, (rep, 1)).T for a in (c, s1, s2))


def _rope_chunk(a, c, s1, s2, half):
    return a * c + pltpu.roll(a, LANES - half, 1) * s1 + pltpu.roll(a, half, 1) * s2


def _store_cols(o_ref, k, val):
    if len(o_ref.shape) == 3:
        o_ref[k] = val.astype(o_ref.dtype)
    else:
        o_ref[:, k * LANES:(k + 1) * LANES] = val.astype(o_ref.dtype)


def _mm_kernel(x_ref, w_ref, o_ref):
    acc = jnp.dot(x_ref[...].astype(BF16), w_ref[...].astype(BF16), preferred_element_type=F32)
    if len(o_ref.shape) == 3:
        for k in range(acc.shape[1] // LANES):
            _store_cols(o_ref, k, acc[:, k * LANES:(k + 1) * LANES])
    else:
        o_ref[...] = acc.astype(o_ref.dtype)


def _mm_rope_kernel(x_ref, w_ref, c_ref, s1_ref, s2_ref, o_ref, *, half, scale0):
    acc = jnp.dot(x_ref[...].astype(BF16), w_ref[...].astype(BF16), preferred_element_type=F32)
    if scale0 is not None:
        acc = acc * jnp.where(pl.program_id(1) == 0, scale0, 1.0)
    c, s1, s2 = c_ref[...], s1_ref[...], s2_ref[...]
    for k in range(acc.shape[1] // LANES):
        _store_cols(o_ref, k, _rope_chunk(acc[:, k * LANES:(k + 1) * LANES], c, s1, s2, half))


def _project(x, w, out_dtype, tables=None, half=0, scale0=None, head_major=False, layer=0, col0=0, n=None,
             tm=1024, tn=1024):
    t, kdim = x.shape
    n = w.shape[-1] if n is None else n
    grid = (t // tm, n // tn)
    x_spec = pl.BlockSpec((tm, kdim), lambda i, j: (i, 0))
    if w.ndim == 3:
        w_spec = pl.BlockSpec((None, kdim, tn), lambda i, j: (layer, 0, col0 + j))
    else:
        w_spec = pl.BlockSpec((kdim, tn), lambda i, j: (0, j))
    if head_major:
        o_spec = pl.BlockSpec((tn // LANES, tm, LANES), lambda i, j: (j, i, 0))
        o_shape = jax.ShapeDtypeStruct((n // LANES, t, LANES), out_dtype)
    else:
        o_spec = pl.BlockSpec((tm, tn), lambda i, j: (i, j))
        o_shape = jax.ShapeDtypeStruct((t, n), out_dtype)
    if tables is None:
        return pl.pallas_call(
            _mm_kernel, grid=grid, in_specs=[x_spec, w_spec], out_specs=o_spec, out_shape=o_shape,
            compiler_params=_params("parallel", "arbitrary"))(x, w)
    t_spec = pl.BlockSpec((tm, LANES), lambda i, j: (i, 0))
    return pl.pallas_call(
        functools.partial(_mm_rope_kernel, half=half, scale0=scale0), grid=grid,
        in_specs=[x_spec, w_spec, t_spec, t_spec, t_spec], out_specs=o_spec, out_shape=o_shape,
        compiler_params=_params("parallel", "arbitrary"))(x, w, *tables)


def _mla_prep_kernel(misc_ref, gq_ref, gkv_ref, wq_ref, wk_ref, wv_ref,
                     cm_ref, s1m_ref, s2m_ref, ci_ref, s1i_ref, s2i_ref,
                     q_ref, k_ref, v_ref, ki_ref, wi_ref):
    tm = misc_ref.shape[0]
    cm, s1m, s2m = cm_ref[...], s1m_ref[...], s2m_ref[...]
    lane = lax.broadcasted_iota(jnp.int32, (tm, LANES), 1)

    cq = misc_ref[:, MISC_CQ:MISC_CQ + Q_LORA]
    cqn = cq * lax.rsqrt(jnp.mean(jnp.square(cq), axis=-1, keepdims=True) + RMS_EPS) * gq_ref[...]
    q_all = jnp.dot(cqn.astype(BF16), wq_ref[...], preferred_element_type=F32) * MLA_QSCALE
    for h in range(MLA_HEADS):
        base = h * MLA_QK_PAD
        q_ref[h, :, 0:MLA_NOPE] = q_all[:, base:base + MLA_NOPE].astype(BF16)
        pe = q_all[:, base + MLA_NOPE:base + MLA_QK_PAD]
        q_ref[h, :, MLA_NOPE:MLA_QK_PAD] = _rope_chunk(pe, cm, s1m, s2m, MLA_ROPE // 2).astype(BF16)

    ckv = misc_ref[:, MISC_CKV:MISC_CKV + KV_LORA]
    ckvn = (ckv * lax.rsqrt(jnp.mean(jnp.square(ckv), axis=-1, keepdims=True) + RMS_EPS) * gkv_ref[...]).astype(BF16)
    kn = jnp.dot(ckvn, wk_ref[...], preferred_element_type=F32)
    vn = jnp.dot(ckvn, wv_ref[...], preferred_element_type=F32)

    krc = misc_ref[:, MISC_KR:MISC_KR + LANES]
    kpe = jnp.where(lane < MLA_ROPE, _rope_chunk(krc, cm, s1m, s2m, MLA_ROPE // 2), 0.0).astype(BF16)
    for h in range(MLA_HEADS):
        k_ref[h, :, 0:MLA_NOPE] = kn[:, h * MLA_NOPE:(h + 1) * MLA_NOPE].astype(BF16)
        k_ref[h, :, MLA_NOPE:MLA_QK_PAD] = kpe
        v_ref[h] = vn[:, h * MLA_V:(h + 1) * MLA_V].astype(BF16)

    kic = misc_ref[:, MISC_KI:MISC_KI + LANES]
    ki_lo = _rope_chunk(kic, ci_ref[...], s1i_ref[...], s2i_ref[...], IDX_ROT_DIM // 2)
    ki_ref[:, 0:LANES] = ki_lo.astype(BF16)
    ki_ref[:, LANES:2 * LANES] = pltpu.roll(ki_lo, IDX_DIM, 1).astype(BF16)
    wi_ref[...] = krc[:, MLA_ROPE:MLA_ROPE + IDX_HEADS] * (IDX_HEADS ** -0.5 * IDX_DIM ** -0.5)


def _mla_prep(misc, gq, gkv, wq, wk, wv, tab_m, tab_i, tm=1024):
    t = misc.shape[0]
    row = lambda w: pl.BlockSpec((tm, w), lambda i: (i, 0))
    full = lambda a: pl.BlockSpec(a.shape, lambda i: (0, 0))
    heads = lambda w: pl.BlockSpec((MLA_HEADS, tm, w), lambda i: (0, i, 0))
    return pl.pallas_call(
        _mla_prep_kernel, grid=(t // tm,),
        in_specs=[row(MISC_W), full(gq), full(gkv), full(wq), full(wk), full(wv)] + [row(LANES)] * 6,
        out_specs=[heads(MLA_QK_PAD), heads(MLA_QK_PAD), heads(MLA_V), row(2 * LANES), row(IDX_HEADS)],
        out_shape=[jax.ShapeDtypeStruct((MLA_HEADS, t, MLA_QK_PAD), BF16),
                   jax.ShapeDtypeStruct((MLA_HEADS, t, MLA_QK_PAD), BF16),
                   jax.ShapeDtypeStruct((MLA_HEADS, t, MLA_V), BF16),
                   jax.ShapeDtypeStruct((t, 2 * LANES), BF16),
                   jax.ShapeDtypeStruct((t, IDX_HEADS), F32)],
        compiler_params=_params("parallel"))(misc, gq, gkv, wq, wk, wv, *tab_m, *tab_i)


def _indexer_kernel(q_ref, w_ref, k_ref, bias_ref, sc_ref, *, tq, tk, topk, max_iter):
    i = pl.program_id(1)
    nk = bias_ref.shape[2]
    nb = (i * tq + tq - 1) // tk + 1
    kf = float(topk)
    qpos = i * tq + lax.broadcasted_iota(jnp.int32, (1, tq), 1)
    kpos0 = lax.broadcasted_iota(jnp.int32, (tk, 1), 0)
    nt = (((1,), (1,)), ((), ()))
    groups = tk // SUBLANES
    fold_rows = 4 * SUBLANES

    def fold_sum(x):
        return jnp.sum(x.reshape(tk // fold_rows, fold_rows, tq), axis=0)

    def score_blk(j, carry):
        mx, mn = carry
        start = pl.multiple_of(j * tk, tk)
        k_lo = k_ref[pl.ds(start, tk), 0:LANES]
        k_hi = k_ref[pl.ds(start, tk), LANES:2 * LANES]
        acc = jnp.zeros((tk, tq), F32)
        for p in range(IDX_HEADS // 2):
            qp = q_ref[:, p * LANES:(p + 1) * LANES]
            s0 = lax.dot_general(k_lo, qp, nt, preferred_element_type=F32)
            s1 = lax.dot_general(k_hi, qp, nt, preferred_element_type=F32)
            acc = acc + jnp.maximum(s0, 0.0) * w_ref[2 * p:2 * p + 1, :]
            acc = acc + jnp.maximum(s1, 0.0) * w_ref[2 * p + 1:2 * p + 2, :]
        causal = (j * tk + kpos0) <= qpos
        acc = acc + 0.0
        sc_ref[j] = jnp.where(causal, acc, -jnp.inf)
        mx = jnp.maximum(mx, jnp.max(jnp.where(causal, acc, -jnp.inf).reshape(groups, SUBLANES, tq), axis=0))
        mn = jnp.minimum(mn, jnp.min(jnp.where(causal, acc, jnp.inf).reshape(groups, SUBLANES, tq), axis=0))
        return mx, mn

    mx8, mn8 = lax.fori_loop(0, nb, score_blk, (jnp.full((SUBLANES, tq), -jnp.inf, F32),
                                                 jnp.full((SUBLANES, tq), jnp.inf, F32)))
    rmax = jnp.max(mx8, axis=0, keepdims=True)
    rmin = jnp.min(mn8, axis=0, keepdims=True)

    def count_ge(th):
        def body(j, c):
            for r in range(tk // fold_rows):
                c = c + jnp.where(sc_ref[j, r * fold_rows:(r + 1) * fold_rows, :] >= th, 1.0, 0.0)
            return c
        part = lax.fori_loop(0, nb, body, jnp.zeros((fold_rows, tq), F32))
        return jnp.sum(part, axis=0, keepdims=True)

    lo = rmin
    c_lo = (qpos + 1).astype(F32)
    hi = rmax + jnp.maximum(jnp.abs(rmax) * 2.0 ** -22, 1e-37)
    c_hi = jnp.zeros((1, tq), F32)
    done = jnp.where(c_lo <= kf, 1.0, 0.0)

    def cond(st):
        it, _, _, _, _, dn = st
        return jnp.logical_and(it < max_iter, jnp.min(dn) < 0.5)

    def halve(st):
        it, lo, hi, c_lo, c_hi, dn = st
        th = 0.5 * lo + 0.5 * hi
        stuck = (th <= lo) | (th >= hi)
        c = count_ge(th)
        active = jnp.logical_not(stuck) & (dn < 0.5)
        up = (c >= kf) & active
        down = (c < kf) & active
        lo = jnp.where(up, th, lo)
        c_lo = jnp.where(up, c, c_lo)
        hi = jnp.where(down, th, hi)
        c_hi = jnp.where(down, c, c_hi)
        dn = jnp.where(stuck | (c_lo <= kf), 1.0, dn)
        return it + 1, lo, hi, c_lo, c_hi, dn

    def step(st):
        return halve(halve(st))

    _, lo, hi, c_lo, c_hi, _ = lax.while_loop(cond, step, (jnp.int32(0), lo, hi, c_lo, c_hi, done))

    need = kf - c_hi
    last_key = jnp.full((1, tq), nk * tk - 1, jnp.int32)

    def band_prefix(kcut):
        def body(j, c):
            s = sc_ref[j]
            hit = (s >= lo) & (s < hi) & ((j * tk + kpos0) < kcut)
            return c + fold_sum(jnp.where(hit, 1.0, 0.0))
        part = lax.fori_loop(0, nb, body, jnp.zeros((fold_rows, tq), F32))
        return jnp.sum(part, axis=0, keepdims=True)

    def tie_search(_):
        nbits = int(nk * tk - 1).bit_length()
        kcut = jnp.zeros((1, tq), jnp.int32)
        for b in range(nbits - 1, -1, -1):
            trial = kcut | (1 << b)
            kcut = jnp.where(band_prefix(trial) < need, trial, kcut)
        return kcut

    has_excess = jnp.max(c_lo) > kf
    kcut = lax.cond(has_excess, tie_search, lambda _: last_key, 0)
    kcut = jnp.where(c_lo > kf, kcut, last_key)

    def emit(j, carry):
        s = sc_ref[j]
        sel = (s >= lo) & ((s >= hi) | ((j * tk + kpos0) <= kcut))
        bias_ref[0, 0, j] = jnp.where(sel, 0.0, MASK_NEG).T.astype(bias_ref.dtype)
        return carry

    lax.fori_loop(0, nb, emit, 0)

    def fill(j, carry):
        bias_ref[0, 0, j] = jnp.full((tq, tk), MASK_NEG, bias_ref.dtype)
        return carry

    lax.fori_loop(nb, nk, fill, 0)


def _indexer(qi, wi_t, ki2, batch, seq, topk, tq=ATTN_TQ, tk=ATTN_TK):
    nq, nk = seq // tq, seq // tk
    kern = functools.partial(_indexer_kernel, tq=tq, tk=tk, topk=topk, max_iter=128)
    return pl.pallas_call(
        kern, grid=(batch, nq),
        in_specs=[pl.BlockSpec((tq, IDX_HEADS * IDX_DIM), lambda b, i: (b * nq + i, 0)),
                  pl.BlockSpec((IDX_HEADS, tq), lambda b, i: (0, b * nq + i)),
                  pl.BlockSpec((seq, 2 * LANES), lambda b, i: (b, 0))],
        out_specs=pl.BlockSpec((1, 1, nk, tq, tk), lambda b, i: (b, i, 0, 0, 0)),
        out_shape=jax.ShapeDtypeStruct((batch, nq, nk, tq, tk), BF16),
        scratch_shapes=[pltpu.VMEM((nk, tk, tq), F32)],
        compiler_params=_params("parallel", "arbitrary"))(qi, wi_t, ki2)


def _attn_kernel(it_ref, jt_ref, fl_ref, q_ref, k_ref, v_ref, *rest, heads, use_bias):
    if use_bias:
        bias_ref, o_ref, m_ref, l_ref, al_ref, acc_ref, s_ref, p_ref, bf_ref = rest
    else:
        o_ref, m_ref, l_ref, al_ref, acc_ref, s_ref, p_ref, bf_ref = rest
    step = pl.program_id(1)
    i = it_ref[step]
    j = jt_ref[step]
    flags = fl_ref[step]
    tq = q_ref.shape[1]
    tk = k_ref.shape[1]
    nt = (((1,), (1,)), ((), ()))

    @pl.when(j == 0)
    def _():
        m_ref[...] = jnp.full(m_ref.shape, -jnp.inf, F32)
        l_ref[...] = jnp.zeros(l_ref.shape, F32)
        acc_ref[...] = jnp.zeros(acc_ref.shape, F32)

    nch = tk // LANES

    def logits(h, slot, biased):
        s = lax.dot_general(q_ref[h], k_ref[h], nt, preferred_element_type=F32)
        if biased:
            s = s + bf_ref[...]
        s_ref[slot] = s

    def softmax(h, slot):
        mc = s_ref[slot, :, 0:LANES]
        for c in range(1, nch):
            mc = jnp.maximum(mc, s_ref[slot, :, c * LANES:(c + 1) * LANES])
        m_prev = m_ref[h]
        m_new = jnp.maximum(m_prev, jnp.max(mc, axis=1, keepdims=True))
        alpha = jnp.exp2(m_prev - m_new)
        m_ref[h] = m_new
        al_ref[h] = alpha
        lsum = None
        for c in range(nch):
            cols = slice(c * LANES, (c + 1) * LANES)
            pc = jnp.exp2(s_ref[slot, :, cols] - m_new)
            lsum = pc if lsum is None else lsum + pc
            p_ref[slot, :, cols] = pc.astype(BF16)
        l_ref[h] = alpha * l_ref[h] + jnp.sum(lsum, axis=1, keepdims=True)

    def weighted_values(h, slot):
        pv = jnp.dot(p_ref[slot], v_ref[h], preferred_element_type=F32)
        acc_ref[h] = al_ref[h] * acc_ref[h] + pv

    def body(biased):
        logits(0, 0, biased)
        logits(1, 1, biased)
        softmax(0, 0)

        for h in range(1, heads - 1):
            cur = h % 2
            logits(h + 1, 1 - cur, biased)
            softmax(h, cur)
            weighted_values(h - 1, 1 - cur)
        last = (heads - 1) % 2
        softmax(heads - 1, last)
        weighted_values(heads - 2, 1 - last)
        weighted_values(heads - 1, last)

    if use_bias:
        bf_ref[...] = bias_ref[0, 0, 0].astype(F32)
        body(True)
    else:
        diag = (flags & 2) != 0

        @pl.when(diag)
        def _():
            row = i * tq + lax.broadcasted_iota(jnp.int32, (tq, 1), 0)
            col = j * tk + lax.broadcasted_iota(jnp.int32, (1, tk), 1)
            bf_ref[...] = jnp.where(col <= row, 0.0, MASK_NEG)
            body(True)

        pl.when(jnp.logical_not(diag))(lambda: body(False))

    @pl.when((flags & 1) != 0)
    def _():
        for h in range(heads):
            o_ref[:, h * LANES:(h + 1) * LANES] = (acc_ref[h] * (1.0 / l_ref[h])).astype(o_ref.dtype)


def _attention(q, k, v, bias, batch, seq, q_head0=0, k_head0=0, heads=8, tq=ATTN_TQ, tk=ATTN_TK):
    dqk, dv = q.shape[2], v.shape[2]
    assert dv == LANES and q_head0 % heads == 0 and k_head0 % heads == 0
    qb, kb = q_head0 // heads, k_head0 // heads
    nq, nk = seq // tq, seq // tk
    pairs = [(i, j) for i in range(nq) for j in range((i * tq + tq - 1) // tk + 1)]
    it = jnp.asarray([p[0] for p in pairs], jnp.int32)
    jt = jnp.asarray([p[1] for p in pairs], jnp.int32)
    fl = jnp.asarray([(1 if j == (i * tq + tq - 1) // tk else 0) + (2 if (j + 1) * tk - 1 > i * tq else 0)
                      for i, j in pairs], jnp.int32)
    use_bias = bias is not None
    in_specs = [pl.BlockSpec((heads, tq, dqk), lambda b, s, it, jt, fl: (qb, b * nq + it[s], 0)),
                pl.BlockSpec((heads, tk, dqk), lambda b, s, it, jt, fl: (kb, b * nk + jt[s], 0)),
                pl.BlockSpec((heads, tk, dv), lambda b, s, it, jt, fl: (0, b * nk + jt[s], 0))]
    args = [q, k, v]
    if use_bias:
        in_specs.append(pl.BlockSpec((1, 1, 1, tq, tk), lambda b, s, it, jt, fl: (b, it[s], jt[s], 0, 0)))
        args.append(bias)
    stat = pltpu.VMEM((heads, tq, LANES), F32)
    scratch = [stat, stat, stat, stat, pltpu.VMEM((2, tq, tk), F32), pltpu.VMEM((2, tq, tk), BF16),
               pltpu.VMEM((tq, tk), F32)]
    return pl.pallas_call(
        functools.partial(_attn_kernel, heads=heads, use_bias=use_bias),
        grid_spec=pltpu.PrefetchScalarGridSpec(
            num_scalar_prefetch=3, grid=(batch, len(pairs)), in_specs=in_specs,
            out_specs=pl.BlockSpec((tq, heads * dv), lambda b, s, it, jt, fl: (b * nq + it[s], 0)),
            scratch_shapes=scratch),
        out_shape=jax.ShapeDtypeStruct((batch * seq, heads * dv), BF16),
        compiler_params=_params("parallel", "arbitrary"))(it, jt, fl, *args)


def _residual_ln(x, upd, g, b, alpha):
    y = alpha * x + upd
    mu = jnp.mean(y, axis=-1, keepdims=True)
    d = y - mu
    var = jnp.mean(jnp.square(d), axis=-1, keepdims=True)
    return d * lax.rsqrt(var + LN_EPS) * g + b


def _outproj_ln_kernel(a_ref, b_ref, wa_ref, wb_ref, x_ref, g_ref, beta_ref, o_ref, ob_ref, *, alpha):
    for r in range(o_ref.shape[0] // LN_PANEL):
        rows = slice(r * LN_PANEL, (r + 1) * LN_PANEL)
        mix = jnp.dot(a_ref[rows, :], wa_ref[...], preferred_element_type=F32)
        mix = mix + jnp.dot(b_ref[rows, :], wb_ref[...], preferred_element_type=F32)
        y = _residual_ln(x_ref[rows, :], mix, g_ref[...], beta_ref[...], alpha)
        o_ref[rows, :] = y
        ob_ref[rows, :] = y.astype(BF16)


def _outproj_ln(a, b, w_o, layer, x, g, beta, alpha, tm=512):
    t, d = x.shape
    wa_rows = a.shape[1]
    row = lambda w: pl.BlockSpec((tm, w), lambda i: (i, 0))
    full = lambda arr: pl.BlockSpec(arr.shape, lambda i: (0, 0))
    half = lambda r: pl.BlockSpec((None, wa_rows, d), lambda i: (layer, r, 0))
    return pl.pallas_call(
        functools.partial(_outproj_ln_kernel, alpha=alpha), grid=(t // tm,),
        in_specs=[row(wa_rows), row(b.shape[1]), half(0), half(1), row(d), full(g), full(beta)],
        out_specs=[row(d), row(d)],
        out_shape=[jax.ShapeDtypeStruct((t, d), F32), jax.ShapeDtypeStruct((t, d), BF16)],
        compiler_params=_params("parallel"))(a, b, w_o, w_o, x, g, beta)


def _ffn_up_kernel(x_ref, wg_ref, wu_ref, h_ref):
    xb = x_ref[...]
    for p in range(h_ref.shape[1] // FFN_PANEL):
        cols = slice(p * FFN_PANEL, (p + 1) * FFN_PANEL)
        g = jnp.dot(xb, wg_ref[:, cols].astype(BF16), preferred_element_type=F32)
        u = jnp.dot(xb, wu_ref[:, cols].astype(BF16), preferred_element_type=F32)
        h_ref[:, cols] = (g * (1.0 / (1.0 + jnp.exp(-g))) * u).astype(h_ref.dtype)


def _ffn_up(x, wg, wu, layer, tm=2048, tn=512):
    t, d = x.shape
    f = wg.shape[2]
    return pl.pallas_call(
        _ffn_up_kernel, grid=(t // tm, f // tn),
        in_specs=[pl.BlockSpec((tm, d), lambda i, j: (i, 0)),
                  pl.BlockSpec((None, d, tn), lambda i, j: (layer, 0, j)),
                  pl.BlockSpec((None, d, tn), lambda i, j: (layer, 0, j))],
        out_specs=pl.BlockSpec((tm, tn), lambda i, j: (i, j)),
        out_shape=jax.ShapeDtypeStruct((t, f), BF16),
        compiler_params=_params("parallel", "arbitrary"))(x, wg, wu)


def _ffn_down_ln_kernel(h_ref, w_ref, x_ref, g_ref, beta_ref, o_ref, *rest, alpha):
    ob_ref, acc_ref = rest if len(rest) == 2 else (None, rest[0])
    k = pl.program_id(1)

    @pl.when(k == 0)
    def _():
        acc_ref[...] = jnp.zeros(acc_ref.shape, F32)

    acc_ref[...] += jnp.dot(h_ref[...], w_ref[...], preferred_element_type=F32)

    @pl.when(k == pl.num_programs(1) - 1)
    def _():
        y = _residual_ln(x_ref[...], acc_ref[...], g_ref[...], beta_ref[...], alpha)
        o_ref[...] = y
        if ob_ref is not None:
            ob_ref[...] = y.astype(BF16)


def _ffn_down_ln(h, w, layer, x, g, beta, alpha, with_copy, tm=512, tk=1408):
    t, d = x.shape
    f = h.shape[1]
    rows = lambda: pl.BlockSpec((tm, d), lambda i, k: (i, 0))
    return pl.pallas_call(
        functools.partial(_ffn_down_ln_kernel, alpha=alpha), grid=(t // tm, f // tk),
        in_specs=[pl.BlockSpec((tm, tk), lambda i, k: (i, k)),
                  pl.BlockSpec((None, tk, d), lambda i, k: (layer, k, 0)),
                  rows(),
                  pl.BlockSpec((1, d), lambda i, k: (0, 0)),
                  pl.BlockSpec((1, d), lambda i, k: (0, 0))],
        out_specs=[rows(), rows()][:2 if with_copy else 1],
        out_shape=[jax.ShapeDtypeStruct((t, d), F32), jax.ShapeDtypeStruct((t, d), BF16)][:2 if with_copy else 1],
        scratch_shapes=[pltpu.VMEM((tm, d), F32)],
        compiler_params=_params("parallel", "arbitrary"))(h, w, x, g, beta)


def _pack_small_weights(w_in, w_uq, w_ukv):
    sl = lambda k: w_in[:, _OFF[k] - _OFF[4]:_OFF[k + 1] - _OFF[4]]
    wi, ki, cq, ckv, kr = (sl(k) for k in range(4, 9))
    d = w_in.shape[0]
    zeros = lambda n: jnp.zeros((d, n), w_in.dtype)
    w_misc = jnp.concatenate([cq, ckv, kr, wi, zeros(LANES - MLA_ROPE - IDX_HEADS), ki, zeros(LANES - IDX_DIM)], axis=1)
    uq = w_uq.reshape(Q_LORA, MLA_HEADS, MLA_NOPE + MLA_ROPE)
    uq = jnp.pad(uq, ((0, 0), (0, 0), (0, MLA_QK_PAD - MLA_NOPE - MLA_ROPE))).reshape(Q_LORA, MLA_HEADS * MLA_QK_PAD)
    ukv = w_ukv.reshape(KV_LORA, MLA_HEADS, MLA_NOPE + MLA_V)
    uk = ukv[:, :, :MLA_NOPE].reshape(KV_LORA, MLA_HEADS * MLA_NOPE)
    uv = ukv[:, :, MLA_NOPE:].reshape(KV_LORA, MLA_HEADS * MLA_V)
    return w_misc, uq, uk, uv


def kernel(x, positions, w_in, g_cq, g_ckv, w_uq, w_ukv, w_o, ln1_g, ln1_b, w_gate, w_up, w_down, ln2_g, ln2_b):
    batch, seq, d = x.shape
    depth = w_in.shape[0]
    alpha = (2 * depth) ** 0.25
    topk = min(TOPK_MAX, seq // 4)
    tab_a = _rope_tables(positions, A_ROT_DIM, ROPE_THETA, A_HEAD_DIM)
    tab_i = _rope_tables(positions, IDX_ROT_DIM, ROPE_THETA, IDX_DIM)
    tab_m = _rope_tables(positions, MLA_ROPE, MLA_ROPE_THETA, LANES)
    xt = x.reshape(batch * seq, d)
    xb = xt
    w_tail_b = w_in[:, :, _OFF[4]:].astype(BF16)
    w_uq_b, w_ukv_b, w_o_b, w_down_b = (w.astype(BF16) for w in (w_uq, w_ukv, w_o, w_down))
    tile = A_WIDTH
    for l in range(depth):
        w_misc, uq, uk, uv = _pack_small_weights(w_tail_b[l], w_uq_b[l], w_ukv_b[l])
        qk = _project(xb, w_in, BF16, tab_a, A_ROT_DIM // 2, scale0=A_QSCALE, head_major=True,
                      layer=l, col0=0, n=2 * tile, tn=tile)
        va = _project(xb, w_in, BF16, head_major=True, layer=l, col0=2, n=tile, tn=tile)
        qi = _project(xb, w_in, BF16, tab_i, IDX_ROT_DIM // 2, layer=l, col0=3, n=tile, tn=tile)
        misc = _project(xb, w_misc, F32)
        q_mla, k_mla, v_mla, ki2, wi = _mla_prep(
            misc, g_cq[l].reshape(1, -1), g_ckv[l].reshape(1, -1), uq, uk, uv, tab_m, tab_i)
        bias = _indexer(qi, wi.T, ki2, batch, seq, topk)
        out_a = _attention(qk, qk, va, bias, batch, seq, q_head0=0, k_head0=A_HEADS, heads=A_HEADS)
        out_b = _attention(q_mla, k_mla, v_mla, None, batch, seq, heads=MLA_HEADS)
        xt, xb = _outproj_ln(out_a, out_b, w_o_b, l, xt,
                             ln1_g[l].reshape(1, -1), ln1_b[l].reshape(1, -1), alpha)
        h = _ffn_up(xb, w_gate, w_up, l)
        outs = _ffn_down_ln(h, w_down_b, l, xt, ln2_g[l].reshape(1, -1), ln2_b[l].reshape(1, -1), alpha,
                            with_copy=l + 1 < depth)
        xt, xb = outs if l + 1 < depth else (outs[0], None)
    return xt.reshape(batch, seq, d)
```
